```python
import math
import jax, jax.numpy as jnp
from jax import lax
import numpy as np

D_MODEL = 1024
BATCH = 8
SEQ = 2048
DEPTH = 4
DEC_BATCH = 2
DEC_SEQ = 16384
PAST_LEN = 128

N_MIXERS = 2
N_MLA_LAYERS = (DEPTH + N_MIXERS - 1) // N_MIXERS
N_NA_LAYERS = DEPTH // N_MIXERS
RMS_EPS = 1e-6
GRID_W = 64
MLA_HEADS = 8
MLA_NOPE = 128
MLA_ROPE = 64
MLA_V = 128
MLA_Q_LORA = 384
MLA_KV_LORA = 256
ROPE_THETA = 10000.0
ATTN_Q_BLOCK = 128
NA_HEADS = 16
NA_HEAD_DIM = D_MODEL // NA_HEADS
NA_KH_MAX = 8
NA_KW = 16
NA_RPB_H = 2 * NA_KH_MAX - 1
NA_RPB_W = 2 * NA_KW - 1
PEER_HEADS = 8
PEER_NKEYS = 128
PEER_N_EXPERTS = PEER_NKEYS * PEER_NKEYS
PEER_QDIM = 256
PEER_TOPK = 16
PEER_CHUNK_MAX = 512

kernel_name = "mla_natten_peer_interleaved_encoder"


def rmsnorm(x, g):
    xf = x.astype(jnp.float32)
    y = xf * lax.rsqrt(jnp.mean(xf * xf, axis=-1, keepdims=True) + RMS_EPS)
    return (y * g.astype(jnp.float32)).astype(x.dtype)


def rope_tables(t_len):
    inv = ROPE_THETA ** (-jnp.arange(0, MLA_ROPE, 2, dtype=jnp.float32) / MLA_ROPE)
    ang = jnp.arange(t_len, dtype=jnp.float32)[:, None] * inv[None, :]
    return jnp.cos(ang), jnp.sin(ang)


def apply_rope(x, cos, sin):
    xf = x.astype(jnp.float32)
    x1, x2 = xf[..., : MLA_ROPE // 2], xf[..., MLA_ROPE // 2:]
    return jnp.concatenate([x1 * cos - x2 * sin, x1 * sin + x2 * cos], axis=-1).astype(x.dtype)


def mla(h, w_dq, q_norm, w_uq, w_dkv, kv_norm, w_ukv, w_o):
    B, T, _ = h.shape
    cq = rmsnorm(h @ w_dq, q_norm)
    q = (cq @ w_uq).reshape(B, T, MLA_HEADS, MLA_NOPE + MLA_ROPE)
    q_nope, q_pe = q[..., :MLA_NOPE], q[..., MLA_NOPE:]
    kv_a = h @ w_dkv
    c_kv = rmsnorm(kv_a[..., :MLA_KV_LORA], kv_norm)
    k_pe = kv_a[..., MLA_KV_LORA:]
    cos, sin = rope_tables(T)
    q_pe = apply_rope(q_pe, cos[:, None, :], sin[:, None, :])
    k_pe = apply_rope(k_pe, cos, sin)
    kv = (c_kv @ w_ukv).reshape(B, T, MLA_HEADS, MLA_NOPE + MLA_V)
    k_nope, v = kv[..., :MLA_NOPE], kv[..., MLA_NOPE:]
    nb = T // ATTN_Q_BLOCK
    scale = (MLA_NOPE + MLA_ROPE) ** -0.5

    def to_blocks(a):
        return a.reshape(B, nb, ATTN_Q_BLOCK, *a.shape[2:]).swapaxes(0, 1)

    def step(args):
        qn, qp = args
        s = (jnp.einsum('bqhd,bkhd->bhqk', qn, k_nope)
             + jnp.einsum('bqhr,bkr->bhqk', qp, k_pe))
        p = jax.nn.softmax(s.astype(jnp.float32) * scale, axis=-1).astype(v.dtype)
        return jnp.einsum('bhqk,bkhd->bqhd', p, v)

    o = lax.map(step, (to_blocks(q_nope), to_blocks(q_pe)))
    o = o.swapaxes(0, 1).reshape(B, T, MLA_HEADS * MLA_V)
    return o @ w_o


def na_indices(rows):
    t_len = rows * GRID_W
    kh = min(NA_KH_MAX, rows)
    t = jnp.arange(t_len, dtype=jnp.int32)
    r, c = t // GRID_W, t % GRID_W
    r0 = jnp.clip(r - kh // 2, 0, rows - kh)
    c0 = jnp.clip(c - NA_KW // 2, 0, GRID_W - NA_KW)
    kr = r0[:, None, None] + jnp.arange(kh, dtype=jnp.int32)[None, :, None]
    kc = c0[:, None, None] + jnp.arange(NA_KW, dtype=jnp.int32)[None, None, :]
    idx = (kr * GRID_W + kc).reshape(t_len, kh * NA_KW)
    rel = ((kr - r[:, None, None] + NA_KH_MAX - 1) * NA_RPB_W
           + (kc - c[:, None, None] + NA_KW - 1)).reshape(t_len, kh * NA_KW)
    return idx, rel


def neighbourhood_attention(h, w_qkv, b_qkv, rpb, w_o):
    B, T, D = h.shape
    rows = T // GRID_W
    qkv = (h @ w_qkv + b_qkv).reshape(B, T, 3, NA_HEADS, NA_HEAD_DIM)
    q = qkv[:, :, 0] * (NA_HEAD_DIM ** -0.5)
    k, v = qkv[:, :, 1], qkv[:, :, 2]
    idx, rel = na_indices(rows)
    rpb_flat = rpb.reshape(NA_HEADS, NA_RPB_H * NA_RPB_W)
    q_rows = q.reshape(B, rows, GRID_W, NA_HEADS, NA_HEAD_DIM).swapaxes(0, 1)

    def step(args):
        q_r, idx_r, rel_r = args
        kg = k[:, idx_r]
        vg = v[:, idx_r]
        s = (jnp.einsum('bqhd,bqkhd->bhqk', q_r, kg).astype(jnp.float32)
             + rpb_flat[:, rel_r].astype(jnp.float32)[None])
        p = jax.nn.softmax(s, axis=-1).astype(vg.dtype)
        return jnp.einsum('bhqk,bqkhd->bqhd', p, vg)

    o = lax.map(step, (q_rows, idx.reshape(rows, GRID_W, -1), rel.reshape(rows, GRID_W, -1)))
    o = o.swapaxes(0, 1).reshape(B, T, D)
    return o @ w_o


def peer(h, w_q, sub_keys, u_tab, v_tab):
    B, T, D = h.shape
    n = B * T
    chunk = math.gcd(n, PEER_CHUNK_MAX)
    xf = h.reshape(n // chunk, chunk, D)

    def step(xc):
        q = (xc @ w_q).reshape(chunk, PEER_HEADS, 2, PEER_QDIM // 2)
        s = jnp.einsum('chpd,hpkd->chpk', q, sub_keys).astype(jnp.float32)
        sv, si = lax.top_k(s, PEER_TOPK)
        cand = (sv[:, :, 0, :, None] + sv[:, :, 1, None, :]).reshape(chunk, PEER_HEADS, PEER_TOPK * PEER_TOPK)
        fv, fi = lax.top_k(cand, PEER_TOPK)
        ia = jnp.take_along_axis(si[:, :, 0], fi // PEER_TOPK, axis=-1)
        ib = jnp.take_along_axis(si[:, :, 1], fi % PEER_TOPK, axis=-1)
        expert = ia * PEER_NKEYS + ib
        g = jax.nn.softmax(fv, axis=-1)
        a = jnp.einsum('cd,chkd->chk', xc, u_tab[expert])
        w = (g * jax.nn.gelu(a.astype(jnp.float32), approximate=False)).astype(xc.dtype)
        return jnp.einsum('chk,chkd->cd', w, v_tab[expert])

    return lax.map(step, xf).reshape(B, T, D)


def trunk(x, norm_mix, norm_ffn, norm_final,
          mla_w_dq, mla_q_norm, mla_w_uq, mla_w_dkv, mla_kv_norm, mla_w_ukv, mla_w_o,
          na_w_qkv, na_b_qkv, na_rpb, na_w_o,
          peer_w_q, peer_sub_keys, peer_u, peer_v):
    for i in range(DEPTH):
        h = rmsnorm(x, norm_mix[i])
        j = i // N_MIXERS
        if i % N_MIXERS == 0:
            mix = mla(h, mla_w_dq[j], mla_q_norm[j], mla_w_uq[j], mla_w_dkv[j],
                      mla_kv_norm[j], mla_w_ukv[j], mla_w_o[j])
        else:
            mix = neighbourhood_attention(h, na_w_qkv[j], na_b_qkv[j], na_rpb[j], na_w_o[j])
        x = x + mix
        h = rmsnorm(x, norm_ffn[i])
        x = x + peer(h, peer_w_q[i], peer_sub_keys[i], peer_u[i], peer_v[i])
    return rmsnorm(x, norm_final)


def setup_inputs(seed: int = 0) -> dict:
    key = jax.random.key(seed)
    ks = jax.random.split(key, 24)
    f32 = jnp.float32

    def nrm(k, shape, scale):
        return jax.random.normal(k, shape, f32) * scale

    def gain(k, shape):
        return 1.0 + 0.02 * jax.random.normal(k, shape, f32)

    D = D_MODEL
    Lm, Ln = N_MLA_LAYERS, N_NA_LAYERS
    return {
        "x_prompt": nrm(ks[0], (BATCH, SEQ, D), 1.0),
        "x_sample": nrm(ks[1], (DEC_BATCH, DEC_SEQ, D), 1.0),
        "norm_mix": gain(ks[2], (DEPTH, D)),
        "norm_ffn": gain(ks[3], (DEPTH, D)),
        "norm_final": gain(ks[4], (D,)),
        "mla_w_dq": nrm(ks[5], (Lm, D, MLA_Q_LORA), D ** -0.5),
        "mla_q_norm": gain(ks[6], (Lm, MLA_Q_LORA)),
        "mla_w_uq": nrm(ks[7], (Lm, MLA_Q_LORA, MLA_HEADS * (MLA_NOPE + MLA_ROPE)), MLA_Q_LORA ** -0.5),
        "mla_w_dkv": nrm(ks[8], (Lm, D, MLA_KV_LORA + MLA_ROPE), D ** -0.5),
        "mla_kv_norm": gain(ks[9], (Lm, MLA_KV_LORA)),
        "mla_w_ukv": nrm(ks[10], (Lm, MLA_KV_LORA, MLA_HEADS * (MLA_NOPE + MLA_V)), MLA_KV_LORA ** -0.5),
        "mla_w_o": nrm(ks[11], (Lm, MLA_HEADS * MLA_V, D), (MLA_HEADS * MLA_V) ** -0.5),
        "na_w_qkv": nrm(ks[12], (Ln, D, 3 * D), D ** -0.5),
        "na_b_qkv": nrm(ks[13], (Ln, 3 * D), 0.02),
        "na_rpb": nrm(ks[14], (Ln, NA_HEADS, NA_RPB_H, NA_RPB_W), 0.1),
        "na_w_o": nrm(ks[15], (Ln, D, D), D ** -0.5),
        "peer_w_q": nrm(ks[16], (DEPTH, D, PEER_HEADS * PEER_QDIM), D ** -0.5),
        "peer_sub_keys": nrm(ks[17], (DEPTH, PEER_HEADS, 2, PEER_NKEYS, PEER_QDIM // 2), (PEER_QDIM // 2) ** -0.5),
        "peer_u": nrm(ks[18], (DEPTH, PEER_N_EXPERTS, D), D ** -0.5),
        "peer_v": nrm(ks[19], (DEPTH, PEER_N_EXPERTS, D), (PEER_HEADS * PEER_TOPK) ** -0.5),
    }


def reference(x_prompt, x_sample, norm_mix, norm_ffn, norm_final,
              mla_w_dq, mla_q_norm, mla_w_uq, mla_w_dkv, mla_kv_norm, mla_w_ukv, mla_w_o,
              na_w_qkv, na_b_qkv, na_rpb, na_w_o,
              peer_w_q, peer_sub_keys, peer_u, peer_v):
    y_prompt = trunk(x_prompt, norm_mix, norm_ffn, norm_final,
                     mla_w_dq, mla_q_norm, mla_w_uq, mla_w_dkv, mla_kv_norm, mla_w_ukv, mla_w_o,
                     na_w_qkv, na_b_qkv, na_rpb, na_w_o,
                     peer_w_q, peer_sub_keys, peer_u, peer_v)
    y_sample = trunk(x_sample, norm_mix, norm_ffn, norm_final,
                     mla_w_dq, mla_q_norm, mla_w_uq, mla_w_dkv, mla_kv_norm, mla_w_ukv, mla_w_o,
                     na_w_qkv, na_b_qkv, na_rpb, na_w_o,
                     peer_w_q, peer_sub_keys, peer_u, peer_v)
    return (y_prompt, y_sample)
```

```python
import functools
import math

import jax
import jax.numpy as jnp
from jax import lax
from jax.experimental import pallas as pl
from jax.experimental.pallas import tpu as pltpu

F32 = jnp.float32
BF16 = jnp.bfloat16

D_MODEL = 1024
DEPTH = 4
RMS_EPS = 1e-6
GRID_W = 64
MLA_HEADS = 8
MLA_NOPE = 128
MLA_ROPE = 64
MLA_V = 128
MLA_Q_LORA = 384
MLA_KV_LORA = 256
MLA_QK_PAD = 256
ROPE_THETA = 10000.0
NA_HEADS = 16
NA_HEAD_DIM = 64
NA_KH = 8
NA_KW = 16
NA_RPB_H = 15
NA_RPB_W = 31
PEER_HEADS = 8
PEER_NKEYS = 128
PEER_TOPK = 16
PEER_SEL = PEER_HEADS * PEER_TOPK
NEG_BIG = -1e30

LANES = 128
VMEM_LIMIT = 56 * 1024 * 1024

NT_DIMS = (((1,), (1,)), ((), ()))


def _cparams(sem):
    return pltpu.CompilerParams(dimension_semantics=sem, vmem_limit_bytes=VMEM_LIMIT)


def _rms(x, g):
    return x * lax.rsqrt(jnp.mean(x * x, axis=-1, keepdims=True) + RMS_EPS) * g


def _full_spec(shape):
    n = len(shape)
    return pl.BlockSpec(shape, lambda *_: (0,) * n)


def _mla_pre_kernel(x_ref, g_ref, wd_ref, qn_ref, kvn_ref, wuq_ref, wukv_ref, c_ref, s_ref,
                    q_out, k_out, v_out, *, scale):
    h = _rms(x_ref[...], g_ref[...]).astype(BF16)
    d = jnp.dot(h, wd_ref[...], preferred_element_type=F32)
    cq = _rms(d[:, :MLA_Q_LORA], qn_ref[...]).astype(BF16)
    ckv = _rms(d[:, MLA_Q_LORA:MLA_Q_LORA + MLA_KV_LORA], kvn_ref[...]).astype(BF16)
    c = c_ref[...]
    s = s_ref[...]
    o = MLA_Q_LORA + MLA_KV_LORA
    kpe = (d[:, o:o + LANES] * c + d[:, o + LANES:o + 2 * LANES] * s).astype(BF16)
    q = jnp.dot(cq, wuq_ref[...], preferred_element_type=F32)
    kv = jnp.dot(ckv, wukv_ref[...], preferred_element_type=F32)
    hw = MLA_HEADS * LANES
    for hh in range(MLA_HEADS):
        lo = hh * MLA_QK_PAD
        sl = slice(hh * LANES, (hh + 1) * LANES)
        q_out[:, lo:lo + LANES] = (q[:, sl] * scale).astype(BF16)
        a = q[:, hw + hh * LANES:hw + (hh + 1) * LANES]
        b = q[:, 2 * hw + hh * LANES:2 * hw + (hh + 1) * LANES]
        q_out[:, lo + LANES:lo + 2 * LANES] = ((a * c + b * s) * scale).astype(BF16)
        k_out[:, lo:lo + LANES] = kv[:, sl].astype(BF16)
        k_out[:, lo + LANES:lo + 2 * LANES] = kpe
    v_out[...] = kv[:, hw:].astype(BF16)


def _mla_pre(x, g, wd, qn, kvn, wuq, wukv, ctab, stab, tm):
    n = x.shape[0]
    scale = float((MLA_NOPE + MLA_ROPE) ** -0.5)
    row = lambda w: pl.BlockSpec((tm, w), lambda i: (i, 0))
    return pl.pallas_call(
        functools.partial(_mla_pre_kernel, scale=scale),
        grid=(n // tm,),
        in_specs=[row(D_MODEL), _full_spec(g.shape), _full_spec(wd.shape), _full_spec(qn.shape),
                  _full_spec(kvn.shape), _full_spec(wuq.shape), _full_spec(wukv.shape),
                  row(LANES), row(LANES)],
        out_specs=[row(MLA_HEADS * MLA_QK_PAD), row(MLA_HEADS * MLA_QK_PAD), row(MLA_HEADS * MLA_V)],
        out_shape=[jax.ShapeDtypeStruct((n, MLA_HEADS * MLA_QK_PAD), BF16),
                   jax.ShapeDtypeStruct((n, MLA_HEADS * MLA_QK_PAD), BF16),
                   jax.ShapeDtypeStruct((n, MLA_HEADS * MLA_V), BF16)],
        compiler_params=_cparams(("arbitrary",)),
        name="mla_pre",
    )(x, g, wd, qn, kvn, wuq, wukv, ctab, stab)


def _flash_kernel(q_ref, k_ref, v_ref, o_ref, *, tk, nk):
    q = q_ref[...]
    tq = q.shape[0]

    def body(j, carry):
        m, l, acc = carry
        start = pl.multiple_of(j * tk, tk)
        ks = k_ref[pl.ds(start, tk), :]
        vs = v_ref[pl.ds(start, tk), :]
        s = lax.dot_general(q, ks, NT_DIMS, preferred_element_type=F32)
        m_new = jnp.maximum(m, jnp.max(s, axis=1, keepdims=True))
        alpha = jnp.exp(m - m_new)
        p = jnp.exp(s - m_new)
        l = alpha * l + jnp.sum(p, axis=1, keepdims=True)
        acc = alpha * acc + jnp.dot(p.astype(BF16), vs, preferred_element_type=F32)
        return m_new, l, acc

    m0 = jnp.full((tq, 1), -jnp.inf, F32)
    l0 = jnp.zeros((tq, 1), F32)
    acc0 = jnp.zeros((tq, MLA_V), F32)
    _, l, acc = lax.fori_loop(0, nk, body, (m0, l0, acc0))
    o_ref[...] = (acc / l).astype(BF16)


def _mla_attention(q, k, v, tok_off, batch, t_len, tq, tk):
    assert tok_off % t_len == 0 and t_len % tq == 0 and t_len % tk == 0
    nq = t_len // tq
    qoff = tok_off // tq
    koff = tok_off // t_len
    return pl.pallas_call(
        functools.partial(_flash_kernel, tk=tk, nk=t_len // tk),
        grid=(batch, MLA_HEADS, nq),
        in_specs=[pl.BlockSpec((tq, MLA_QK_PAD), lambda b, h, i: (qoff + b * nq + i, h)),
                  pl.BlockSpec((t_len, MLA_QK_PAD), lambda b, h, i: (koff + b, h)),
                  pl.BlockSpec((t_len, MLA_V), lambda b, h, i: (koff + b, h))],
        out_specs=pl.BlockSpec((tq, MLA_V), lambda b, h, i: (b * nq + i, h)),
        out_shape=jax.ShapeDtypeStruct((batch * t_len, MLA_HEADS * MLA_V), BF16),
        compiler_params=_cparams(("arbitrary", "arbitrary", "arbitrary")),
        name="mla_flash",
    )(q, k, v)


def _na_pre_kernel(x_ref, g_ref, w_ref, b_ref, q_out, k_out, v_out, *, scale):
    h = _rms(x_ref[...], g_ref[...]).astype(BF16)
    qkv = jnp.dot(h, w_ref[...], preferred_element_type=F32) + b_ref[...]
    q_out[...] = (qkv[:, :D_MODEL] * scale).astype(BF16)
    k_out[...] = qkv[:, D_MODEL:2 * D_MODEL].astype(BF16)
    v_out[...] = qkv[:, 2 * D_MODEL:].astype(BF16)


def _na_pre(x, g, w, b, tm):
    n = x.shape[0]
    row = pl.BlockSpec((tm, D_MODEL), lambda i: (i, 0))
    out = jax.ShapeDtypeStruct((n, D_MODEL), BF16)
    return pl.pallas_call(
        functools.partial(_na_pre_kernel, scale=float(NA_HEAD_DIM ** -0.5)),
        grid=(n // tm,),
        in_specs=[row, _full_spec(g.shape), _full_spec(w.shape), _full_spec(b.shape)],
        out_specs=[row, row, row],
        out_shape=[out, out, out],
        compiler_params=_cparams(("arbitrary",)),
        name="na_pre",
    )(x, g, w, b)


def _na_bias_kernel(rpb_ref, oh_ref, valid_ref, o_ref):
    rpb = rpb_ref[...]
    acc = jnp.zeros(o_ref.shape, F32)
    for d in range(NA_RPB_W):
        acc = jnp.where(oh_ref[d:d + 1, :] > 0, rpb[:, d:d + 1], acc)
    o_ref[...] = jnp.where(valid_ref[...] > 0, acc, NEG_BIG)


def _na_bias_table(rpb):
    c = jnp.arange(GRID_W)[:, None]
    kc = jnp.arange(GRID_W)[None, :]
    rel = (kc - c + NA_KW - 1).reshape(1, GRID_W * GRID_W)
    oh = (rel == jnp.arange(32)[:, None]).astype(F32)
    c0 = jnp.clip(c - NA_KW // 2, 0, GRID_W - NA_KW)
    valid = ((kc >= c0) & (kc < c0 + NA_KW)).astype(F32).reshape(1, GRID_W * GRID_W)
    rpb2 = jnp.pad(rpb.reshape(NA_HEADS * NA_RPB_H, NA_RPB_W), ((0, 0), (0, 1)))
    toep = pl.pallas_call(
        _na_bias_kernel,
        out_shape=jax.ShapeDtypeStruct((NA_HEADS * NA_RPB_H, GRID_W * GRID_W), F32),
        name="na_bias",
    )(rpb2, oh, valid)
    toep = toep.reshape(NA_HEADS, NA_RPB_H, GRID_W, GRID_W)
    tabs = []
    for base in range(NA_KH):
        t = toep[:, base:base + NA_KH]
        tabs.append(t.transpose(0, 2, 1, 3).reshape(NA_HEADS * GRID_W, NA_KH * GRID_W))
    return jnp.stack(tabs)


def _na_attn_kernel(*refs):
    q_ref = refs[0]
    k_refs = refs[1:1 + NA_KH]
    v_refs = refs[1 + NA_KH:1 + 2 * NA_KH]
    b_ref = refs[1 + 2 * NA_KH]
    o_ref = refs[2 + 2 * NA_KH]
    lane = lax.broadcasted_iota(jnp.int32, (GRID_W, LANES), 1)
    lo = lane < NA_HEAD_DIM
    zero = jnp.zeros((GRID_W, LANES), BF16)
    for hp in range(NA_HEADS // 2):
        sl = slice(hp * LANES, (hp + 1) * LANES)
        q2 = q_ref[:, sl]
        qq = jnp.concatenate([jnp.where(lo, q2, zero), jnp.where(lo, zero, q2)], axis=0)
        kw = jnp.concatenate([r[:, sl] for r in k_refs], axis=0)
        vw = jnp.concatenate([r[:, sl] for r in v_refs], axis=0)
        s = lax.dot_general(qq, kw, NT_DIMS, preferred_element_type=F32) + b_ref[0, sl, :]
        m = jnp.max(s, axis=1, keepdims=True)
        p = jnp.exp(s - m)
        l = jnp.sum(p, axis=1, keepdims=True)
        o = jnp.dot(p.astype(BF16), vw, preferred_element_type=F32) / l
        o_ref[:, sl] = jnp.where(lo, o[:GRID_W], o[GRID_W:]).astype(BF16)


def _na_attention(q, k, v, bias, tok_off, batch, t_len):
    rows = t_len // GRID_W
    assert rows >= NA_KH and tok_off % GRID_W == 0
    roff = tok_off // GRID_W

    def r0(r):
        return jnp.clip(r - NA_KH // 2, 0, rows - NA_KH)

    blk = (GRID_W, D_MODEL)
    q_spec = pl.BlockSpec(blk, lambda b, r: (roff + b * rows + r, 0))
    kv_specs = [pl.BlockSpec(blk, functools.partial(lambda b, r, j: (roff + b * rows + r0(r) + j, 0), j=j))
                for j in range(NA_KH)]
    b_spec = pl.BlockSpec((1,) + bias.shape[1:], lambda b, r: (r0(r) - r + NA_KH - 1, 0, 0))
    return pl.pallas_call(
        _na_attn_kernel,
        grid=(batch, rows),
        in_specs=[q_spec] + kv_specs + kv_specs + [b_spec],
        out_specs=pl.BlockSpec(blk, lambda b, r: (b * rows + r, 0)),
        out_shape=jax.ShapeDtypeStruct((batch * t_len, D_MODEL), BF16),
        compiler_params=_cparams(("arbitrary", "arbitrary")),
        name="na_attn",
    )(q, *([k] * NA_KH), *([v] * NA_KH), bias)


def _mix_post_kernel(x_ref, o_ref, wo_ref, g_ref, wq_ref, x_out, h_out, q_out):
    x = x_ref[...] + jnp.dot(o_ref[...], wo_ref[...], preferred_element_type=F32)
    x_out[...] = x
    h = _rms(x, g_ref[...]).astype(BF16)
    h_out[...] = h
    q = jnp.dot(h, wq_ref[...], preferred_element_type=F32).astype(BF16)
    for c in range(q_out.shape[0]):
        q_out[c] = q[:, c * LANES:(c + 1) * LANES]


def _mix_post(x, o, wo, g, wq, tm):
    n = x.shape[0]
    row = lambda w: pl.BlockSpec((tm, w), lambda i: (i, 0))
    nq = wq.shape[1] // LANES
    return pl.pallas_call(
        _mix_post_kernel,
        grid=(n // tm,),
        in_specs=[row(D_MODEL), row(D_MODEL), _full_spec(wo.shape), _full_spec(g.shape), _full_spec(wq.shape)],
        out_specs=[row(D_MODEL), row(D_MODEL), pl.BlockSpec((nq, tm, LANES), lambda i: (0, i, 0))],
        out_shape=[jax.ShapeDtypeStruct((n, D_MODEL), F32),
                   jax.ShapeDtypeStruct((n, D_MODEL), BF16),
                   jax.ShapeDtypeStruct((nq, n, LANES), BF16)],
        compiler_params=_cparams(("arbitrary",)),
        name="mix_post",
    )(x, o, wo, g, wq)


def _top16(s, iota):
    n = s.shape[0]
    vals, idxs = [], []
    for _ in range(PEER_TOPK):
        m = jnp.max(s, axis=0, keepdims=True)
        idx = jnp.min(jnp.where(s == m, iota, n), axis=0, keepdims=True)
        vals.append(m)
        idxs.append(idx)
        s = jnp.where(iota == idx, -jnp.inf, s)
    return vals, idxs


def _peer_route_kernel(q_ref, keys_ref, ia_out, ib_out, g_out, ia_scr, ib_scr, g_scr):
    tm = q_ref.shape[1]
    iota_k = lax.broadcasted_iota(jnp.int32, (PEER_NKEYS, tm), 0)
    iota_c = lax.broadcasted_iota(jnp.int32, (PEER_TOPK * PEER_TOPK, tm), 0)

    def head(h, _):
        halves = []
        for p in range(2):
            qh = q_ref[h * 2 + p]
            st = lax.dot_general(keys_ref[h * 2 + p], qh, NT_DIMS, preferred_element_type=F32)
            halves.append(_top16(st, iota_k))
        (va, ia), (vb, ib) = halves
        vb_all = jnp.concatenate(vb, axis=0)
        cand = jnp.concatenate([va[i] + vb_all for i in range(PEER_TOPK)], axis=0)
        fv, fi = _top16(cand, iota_c)
        fv = jnp.concatenate(fv, axis=0)
        fi = jnp.concatenate(fi, axis=0)
        fa = fi // PEER_TOPK
        fb = fi % PEER_TOPK
        sel_a = jnp.zeros_like(fi)
        sel_b = jnp.zeros_like(fi)
        for i in range(PEER_TOPK):
            sel_a = jnp.where(fa == i, ia[i], sel_a)
            sel_b = jnp.where(fb == i, ib[i], sel_b)
        e = jnp.exp(fv - fv[0:1])
        g = e / jnp.sum(e, axis=0, keepdims=True)
        row = pl.multiple_of(h * PEER_TOPK, PEER_TOPK)
        ia_scr[pl.ds(row, PEER_TOPK), :] = sel_a
        ib_scr[pl.ds(row, PEER_TOPK), :] = sel_b
        g_scr[pl.ds(row, PEER_TOPK), :] = g
        return 0

    lax.fori_loop(0, PEER_HEADS, head, 0)
    ia_out[...] = ia_scr[...].T
    ib_out[...] = ib_scr[...].T
    g_out[...] = g_scr[...].T


def _peer_route(qp, keys, tm):
    n = qp.shape[1]
    row = lambda w: pl.BlockSpec((tm, w), lambda i: (i, 0))
    return pl.pallas_call(
        _peer_route_kernel,
        grid=(n // tm,),
        in_specs=[pl.BlockSpec((qp.shape[0], tm, LANES), lambda i: (0, i, 0)), _full_spec(keys.shape)],
        out_specs=[row(PEER_SEL), row(PEER_SEL), row(PEER_SEL)],
        out_shape=[jax.ShapeDtypeStruct((n, PEER_SEL), jnp.int32),
                   jax.ShapeDtypeStruct((n, PEER_SEL), jnp.int32),
                   jax.ShapeDtypeStruct((n, PEER_SEL), F32)],
        scratch_shapes=[pltpu.VMEM((PEER_SEL, tm), jnp.int32),
                        pltpu.VMEM((PEER_SEL, tm), jnp.int32),
                        pltpu.VMEM((PEER_SEL, tm), F32)],
        compiler_params=_cparams(("arbitrary",)),
        name="peer_route",
    )(qp, keys)


def _peer_dense_kernel(h_ref, ia_ref, ib_ref, g_ref, x_ref, u_ref, v_ref, o_ref, w_scr, acc_ref, *, eb):
    j = pl.program_id(1)
    tm = h_ref.shape[0]
    nblk = eb // PEER_NKEYS

    @pl.when(j == 0)
    def _():
        acc_ref[...] = jnp.zeros_like(acc_ref)
        sub = lax.broadcasted_iota(jnp.int32, (PEER_NKEYS, PEER_SEL), 0)

        def build(c, _):
            ia = ia_ref[pl.ds(c, 1), :]
            ib = ib_ref[pl.ds(c, 1), :]
            g = g_ref[pl.ds(c, 1), :]
            pt = jnp.where(sub == ia, 1.0, 0.0).astype(BF16)
            qt = jnp.where(sub == ib, g, 0.0).astype(BF16)
            w = lax.dot_general(pt, qt, NT_DIMS, preferred_element_type=F32)
            w_scr[pl.ds(pl.multiple_of(c * PEER_NKEYS, PEER_NKEYS), PEER_NKEYS), :] = w
            return 0

        lax.fori_loop(0, tm, build, 0)

    a = lax.dot_general(h_ref[...], u_ref[...], NT_DIMS, preferred_element_type=F32)
    gl = 0.5 * a * (1.0 + lax.erf(a * (2.0 ** -0.5)))
    parts = []
    for k in range(nblk):
        wk = w_scr[pl.ds(j * nblk + k, tm, stride=PEER_NKEYS), :]
        parts.append((gl[:, k * PEER_NKEYS:(k + 1) * PEER_NKEYS] * wk).astype(BF16))
    wg = jnp.concatenate(parts, axis=1)
    acc_ref[...] += jnp.dot(wg, v_ref[...], preferred_element_type=F32)

    @pl.when(j == pl.num_programs(1) - 1)
    def _():
        o_ref[...] = x_ref[...] + acc_ref[...]


def _peer_dense(h, ia, ib, g, x, u, v, tm, eb):
    n = h.shape[0]
    ne = u.shape[0]
    row = lambda w: pl.BlockSpec((tm, w), lambda i, j: (i, 0))
    return pl.pallas_call(
        functools.partial(_peer_dense_kernel, eb=eb),
        grid=(n // tm, ne // eb),
        in_specs=[row(D_MODEL), row(PEER_SEL), row(PEER_SEL), row(PEER_SEL), row(D_MODEL),
                  pl.BlockSpec((eb, D_MODEL), lambda i, j: (j, 0)),
                  pl.BlockSpec((eb, D_MODEL), lambda i, j: (j, 0))],
        out_specs=row(D_MODEL),
        out_shape=jax.ShapeDtypeStruct((n, D_MODEL), F32),
        scratch_shapes=[pltpu.VMEM((tm * PEER_NKEYS, PEER_NKEYS), F32),
                        pltpu.VMEM((tm, D_MODEL), F32)],
        compiler_params=_cparams(("arbitrary", "arbitrary")),
        name="peer_dense",
    )(h, ia, ib, g, x, u, v)


def _final_norm_kernel(x_ref, g_ref, o_ref):
    o_ref[...] = _rms(x_ref[...], g_ref[...])


def _final_norm(x, g, tm):
    n = x.shape[0]
    row = pl.BlockSpec((tm, D_MODEL), lambda i: (i, 0))
    return pl.pallas_call(
        _final_norm_kernel,
        grid=(n // tm,),
        in_specs=[row, _full_spec(g.shape)],
        out_specs=row,
        out_shape=jax.ShapeDtypeStruct((n, D_MODEL), F32),
        compiler_params=_cparams(("arbitrary",)),
        name="final_norm",
    )(x, g)


def _swap_halves(w):
    half = MLA_ROPE // 2
    return jnp.concatenate([w[..., half:], w[..., :half]], axis=-1)


def _pad_lanes(w):
    return jnp.pad(w, [(0, 0)] * (w.ndim - 1) + [(0, LANES - w.shape[-1])])


def _mla_weights(w_dq, w_uq, w_dkv, w_ukv):
    kpe = w_dkv[:, MLA_KV_LORA:]
    wd = jnp.concatenate([w_dq, w_dkv[:, :MLA_KV_LORA], _pad_lanes(kpe), _pad_lanes(_swap_halves(kpe))], axis=1)
    uq = w_uq.reshape(MLA_Q_LORA, MLA_HEADS, MLA_NOPE + MLA_ROPE)
    pe = uq[:, :, MLA_NOPE:]
    flat = lambda a: a.reshape(MLA_Q_LORA, -1)
    wuq = jnp.concatenate([flat(uq[:, :, :MLA_NOPE]), flat(_pad_lanes(pe)), flat(_pad_lanes(_swap_halves(pe)))], axis=1)
    ukv = w_ukv.reshape(MLA_KV_LORA, MLA_HEADS, MLA_NOPE + MLA_V)
    wukv = jnp.concatenate([ukv[:, :, :MLA_NOPE].reshape(MLA_KV_LORA, -1),
                            ukv[:, :, MLA_NOPE:].reshape(MLA_KV_LORA, -1)], axis=1)
    return wd.astype(BF16), wuq.astype(BF16), wukv.astype(BF16)


def _rope_tables(segments):
    inv = ROPE_THETA ** (-jnp.arange(0, MLA_ROPE, 2, dtype=F32) / MLA_ROPE)
    cs, ss = [], []
    for batch, t_len in segments:
        ang = jnp.arange(t_len, dtype=F32)[:, None] * inv[None, :]
        cos, sin = jnp.cos(ang), jnp.sin(ang)
        cs.append(jnp.tile(_pad_lanes(jnp.concatenate([cos, cos], axis=1)), (batch, 1)))
        ss.append(jnp.tile(_pad_lanes(jnp.concatenate([-sin, sin], axis=1)), (batch, 1)))
    return jnp.concatenate(cs), jnp.concatenate(ss)


def _pick_tile(n, pref):
    t = pref
    while n % t:
        t //= 2
    return t


def _trunk(x_prompt, x_sample, norm_mix, norm_ffn, norm_final,
           mla_w_dq, mla_q_norm, mla_w_uq, mla_w_dkv, mla_kv_norm, mla_w_ukv, mla_w_o,
           na_w_qkv, na_b_qkv, na_rpb, na_w_o, peer_w_q, peer_sub_keys, peer_u, peer_v):
    segs = [(x_prompt.shape[0], x_prompt.shape[1]), (x_sample.shape[0], x_sample.shape[1])]
    offs = [0, segs[0][0] * segs[0][1]]
    x = jnp.concatenate([x_prompt.reshape(-1, D_MODEL), x_sample.reshape(-1, D_MODEL)], axis=0)
    n = x.shape[0]
    tm = _pick_tile(n, 256)
    ctab, stab = _rope_tables(segs)
    row2 = lambda a: a.reshape(1, -1)

    for i in range(DEPTH):
        j = i // 2
        if i % 2 == 0:
            wd, wuq, wukv = _mla_weights(mla_w_dq[j], mla_w_uq[j], mla_w_dkv[j], mla_w_ukv[j])
            q, k, v = _mla_pre(x, row2(norm_mix[i]), wd, row2(mla_q_norm[j]), row2(mla_kv_norm[j]),
                               wuq, wukv, ctab, stab, tm)
            o = jnp.concatenate([
                _mla_attention(q, k, v, off, b, t, _pick_tile(t, 512), _pick_tile(t, 512))
                for off, (b, t) in zip(offs, segs)], axis=0)
            wo = mla_w_o[j].astype(BF16)
        else:
            q, k, v = _na_pre(x, row2(norm_mix[i]), na_w_qkv[j].astype(BF16), row2(na_b_qkv[j]), tm)
            bias = _na_bias_table(na_rpb[j])
            o = jnp.concatenate([
                _na_attention(q, k, v, bias, off, b, t) for off, (b, t) in zip(offs, segs)], axis=0)
            wo = na_w_o[j].astype(BF16)
        x, h, qp = _mix_post(x, o, wo, row2(norm_ffn[i]), peer_w_q[i].astype(BF16), tm)
        keys = peer_sub_keys[i].reshape(PEER_HEADS * 2, PEER_NKEYS, -1).astype(BF16)
        ia, ib, g = _peer_route(qp, keys, tm)
        x = _peer_dense(h, ia, ib, g, x, peer_u[i].astype(BF16), peer_v[i].astype(BF16), tm, 1024)

    y = _final_norm(x, row2(norm_final), tm)
    return y[:offs[1]].reshape(x_prompt.shape), y[offs[1]:].reshape(x_sample.shape)


def kernel(x_prompt, x_sample, norm_mix, norm_ffn, norm_final, mla_w_dq, mla_q_norm, mla_w_uq, mla_w_dkv, mla_kv_norm, mla_w_ukv, mla_w_o, na_w_qkv, na_b_qkv, na_rpb, na_w_o, peer_w_q, peer_sub_keys, peer_u, peer_v):
    return _trunk(x_prompt, x_sample, norm_mix, norm_ffn, norm_final,
                  mla_w_dq, mla_q_norm, mla_w_uq, mla_w_dkv, mla_kv_norm, mla_w_ukv, mla_w_o,
                  na_w_qkv, na_b_qkv, na_rpb, na_w_o, peer_w_q, peer_sub_keys, peer_u, peer_v)
```

```python
import functools
import math

import jax
import jax.numpy as jnp
from jax import lax
from jax.experimental import pallas as pl
from jax.experimental.pallas import tpu as pltpu

F32 = jnp.float32
BF16 = jnp.bfloat16

D_MODEL = 1024
DEPTH = 4
RMS_EPS = 1e-6
GRID_W = 64
MLA_HEADS = 8
MLA_NOPE = 128
MLA_ROPE = 64
MLA_V = 128
MLA_Q_LORA = 384
MLA_KV_LORA = 256
MLA_QK_PAD = 256
ROPE_THETA = 10000.0
NA_HEADS = 16
NA_HEAD_DIM = 64
NA_KH = 8
NA_KW = 16
NA_RPB_H = 15
NA_RPB_W = 31
PEER_HEADS = 8
PEER_NKEYS = 128
PEER_TOPK = 16
PEER_SEL = PEER_HEADS * PEER_TOPK
NEG_BIG = -1e30

LANES = 128
VMEM_LIMIT = 56 * 1024 * 1024

NT_DIMS = (((1,), (1,)), ((), ()))


def _cparams(sem):
    return pltpu.CompilerParams(dimension_semantics=sem, vmem_limit_bytes=VMEM_LIMIT)


def _rms(x, g):
    return x * lax.rsqrt(jnp.mean(x * x, axis=-1, keepdims=True) + RMS_EPS) * g


def _full_spec(shape):
    n = len(shape)
    return pl.BlockSpec(shape, lambda *_: (0,) * n)


def _mla_pre_kernel(x_ref, g_ref, wd_ref, qn_ref, kvn_ref, wuq_ref, wukv_ref, c_ref, s_ref,
                    q_out, k_out, v_out, *, scale):
    h = _rms(x_ref[...], g_ref[...]).astype(BF16)
    d = jnp.dot(h, wd_ref[...], preferred_element_type=F32)
    cq = _rms(d[:, :MLA_Q_LORA], qn_ref[...]).astype(BF16)
    ckv = _rms(d[:, MLA_Q_LORA:MLA_Q_LORA + MLA_KV_LORA], kvn_ref[...]).astype(BF16)
    c = c_ref[...]
    s = s_ref[...]
    o = MLA_Q_LORA + MLA_KV_LORA
    kpe = (d[:, o:o + LANES] * c + d[:, o + LANES:o + 2 * LANES] * s).astype(BF16)
    q = jnp.dot(cq, wuq_ref[...], preferred_element_type=F32)
    kv = jnp.dot(ckv, wukv_ref[...], preferred_element_type=F32)
    hw = MLA_HEADS * LANES
    for hh in range(MLA_HEADS):
        lo = hh * MLA_QK_PAD
        sl = slice(hh * LANES, (hh + 1) * LANES)
        q_out[:, lo:lo + LANES] = (q[:, sl] * scale).astype(BF16)
        a = q[:, hw + hh * LANES:hw + (hh + 1) * LANES]
        b = q[:, 2 * hw + hh * LANES:2 * hw + (hh + 1) * LANES]
        q_out[:, lo + LANES:lo + 2 * LANES] = ((a * c + b * s) * scale).astype(BF16)
        k_out[:, lo:lo + LANES] = kv[:, sl].astype(BF16)
        k_out[:, lo + LANES:lo + 2 * LANES] = kpe
    v_out[...] = kv[:, hw:].astype(BF16)


def _mla_pre(x, g, wd, qn, kvn, wuq, wukv, ctab, stab, tm):
    n = x.shape[0]
    scale = float((MLA_NOPE + MLA_ROPE) ** -0.5 * math.log2(math.e))
    row = lambda w: pl.BlockSpec((tm, w), lambda i: (i, 0))
    return pl.pallas_call(
        functools.partial(_mla_pre_kernel, scale=scale),
        grid=(n // tm,),
        in_specs=[row(D_MODEL), _full_spec(g.shape), _full_spec(wd.shape), _full_spec(qn.shape),
                  _full_spec(kvn.shape), _full_spec(wuq.shape), _full_spec(wukv.shape),
                  row(LANES), row(LANES)],
        out_specs=[row(MLA_HEADS * MLA_QK_PAD), row(MLA_HEADS * MLA_QK_PAD), row(MLA_HEADS * MLA_V)],
        out_shape=[jax.ShapeDtypeStruct((n, MLA_HEADS * MLA_QK_PAD), BF16),
                   jax.ShapeDtypeStruct((n, MLA_HEADS * MLA_QK_PAD), BF16),
                   jax.ShapeDtypeStruct((n, MLA_HEADS * MLA_V), BF16)],
        compiler_params=_cparams(("arbitrary",)),
        name="mla_pre",
    )(x, g, wd, qn, kvn, wuq, wukv, ctab, stab)


def _flash_kernel(q_ref, k_ref, v_ref, o_ref, sa_ref, sb_ref, *, tk, nk):
    q = q_ref[...]
    tq = q.shape[0]

    def scores(j, dst):
        start = pl.multiple_of(j * tk, tk)
        dst[...] = lax.dot_general(q, k_ref[pl.ds(start, tk), :], NT_DIMS, preferred_element_type=F32)

    def update(j, src, m, l, acc):
        start = pl.multiple_of(j * tk, tk)
        vs = v_ref[pl.ds(start, tk), :]
        s = src[...]
        m_new = jnp.maximum(m, jnp.max(s, axis=1, keepdims=True))
        alpha = jnp.exp2(m - m_new)
        p = jnp.exp2(s - m_new)
        l = alpha * l + jnp.sum(p, axis=1, keepdims=True)
        acc = alpha * acc + jnp.dot(p.astype(BF16), vs, preferred_element_type=F32)
        return m_new, l, acc

    def body(jj, carry):
        m, l, acc = carry
        j = 2 * jj
        scores(j + 1, sb_ref)
        m, l, acc = update(j, sa_ref, m, l, acc)
        scores(jnp.minimum(j + 2, nk - 1), sa_ref)
        m, l, acc = update(j + 1, sb_ref, m, l, acc)
        return m, l, acc

    m0 = jnp.full((tq, 1), -jnp.inf, F32)
    l0 = jnp.zeros((tq, 1), F32)
    acc0 = jnp.zeros((tq, MLA_V), F32)
    scores(0, sa_ref)
    _, l, acc = lax.fori_loop(0, nk // 2, body, (m0, l0, acc0))
    o_ref[...] = (acc / l).astype(BF16)


def _mla_attention(q, k, v, tok_off, batch, t_len, tq, tk):
    assert tok_off % t_len == 0 and t_len % tq == 0 and t_len % (2 * tk) == 0
    nq = t_len // tq
    qoff = tok_off // tq
    koff = tok_off // t_len
    return pl.pallas_call(
        functools.partial(_flash_kernel, tk=tk, nk=t_len // tk),
        grid=(batch, MLA_HEADS, nq),
        scratch_shapes=[pltpu.VMEM((tq, tk), F32), pltpu.VMEM((tq, tk), F32)],
        in_specs=[pl.BlockSpec((tq, MLA_QK_PAD), lambda b, h, i: (qoff + b * nq + i, h)),
                  pl.BlockSpec((t_len, MLA_QK_PAD), lambda b, h, i: (koff + b, h)),
                  pl.BlockSpec((t_len, MLA_V), lambda b, h, i: (koff + b, h))],
        out_specs=pl.BlockSpec((tq, MLA_V), lambda b, h, i: (b * nq + i, h)),
        out_shape=jax.ShapeDtypeStruct((batch * t_len, MLA_HEADS * MLA_V), BF16),
        compiler_params=_cparams(("arbitrary", "arbitrary", "arbitrary")),
        name="mla_flash",
    )(q, k, v)


def _na_pre_kernel(x_ref, g_ref, w_ref, b_ref, q_out, k_out, v_out, *, scale):
    h = _rms(x_ref[...], g_ref[...]).astype(BF16)
    qkv = jnp.dot(h, w_ref[...], preferred_element_type=F32) + b_ref[...]
    q_out[...] = (qkv[:, :D_MODEL] * scale).astype(BF16)
    k_out[...] = qkv[:, D_MODEL:2 * D_MODEL].astype(BF16)
    v_out[...] = qkv[:, 2 * D_MODEL:].astype(BF16)


def _na_pre(x, g, w, b, tm):
    n = x.shape[0]
    row = pl.BlockSpec((tm, D_MODEL), lambda i: (i, 0))
    out = jax.ShapeDtypeStruct((n, D_MODEL), BF16)
    return pl.pallas_call(
        functools.partial(_na_pre_kernel, scale=float(NA_HEAD_DIM ** -0.5)),
        grid=(n // tm,),
        in_specs=[row, _full_spec(g.shape), _full_spec(w.shape), _full_spec(b.shape)],
        out_specs=[row, row, row],
        out_shape=[out, out, out],
        compiler_params=_cparams(("arbitrary",)),
        name="na_pre",
    )(x, g, w, b)


def _na_bias_kernel(rpb_ref, oh_ref, valid_ref, o_ref):
    rpb = rpb_ref[...]
    acc = jnp.zeros(o_ref.shape, F32)
    for d in range(NA_RPB_W):
        acc = jnp.where(oh_ref[d:d + 1, :] > 0, rpb[:, d:d + 1], acc)
    o_ref[...] = jnp.where(valid_ref[...] > 0, acc, NEG_BIG)


def _na_bias_table(rpb):
    c = jnp.arange(GRID_W)[:, None]
    kc = jnp.arange(GRID_W)[None, :]
    rel = (kc - c + NA_KW - 1).reshape(1, GRID_W * GRID_W)
    oh = (rel == jnp.arange(32)[:, None]).astype(F32)
    c0 = jnp.clip(c - NA_KW // 2, 0, GRID_W - NA_KW)
    valid = ((kc >= c0) & (kc < c0 + NA_KW)).astype(F32).reshape(1, GRID_W * GRID_W)
    rpb2 = jnp.pad(rpb.reshape(NA_HEADS * NA_RPB_H, NA_RPB_W), ((0, 0), (0, 1)))
    toep = pl.pallas_call(
        _na_bias_kernel,
        out_shape=jax.ShapeDtypeStruct((NA_HEADS * NA_RPB_H, GRID_W * GRID_W), F32),
        name="na_bias",
    )(rpb2, oh, valid)
    toep = toep.reshape(NA_HEADS, NA_RPB_H, GRID_W, GRID_W)
    tabs = []
    for base in range(NA_KH):
        t = toep[:, base:base + NA_KH]
        tabs.append(t.transpose(0, 2, 1, 3).reshape(NA_HEADS * GRID_W, NA_KH * GRID_W))
    return jnp.stack(tabs)


def _na_attn_kernel(*refs):
    q_ref = refs[0]
    k_refs = refs[1:1 + NA_KH]
    v_refs = refs[1 + NA_KH:1 + 2 * NA_KH]
    b_ref = refs[1 + 2 * NA_KH]
    o_ref = refs[2 + 2 * NA_KH]
    lane = lax.broadcasted_iota(jnp.int32, (GRID_W, LANES), 1)
    lo = lane < NA_HEAD_DIM
    zero = jnp.zeros((GRID_W, LANES), BF16)
    for hp in range(NA_HEADS // 2):
        sl = slice(hp * LANES, (hp + 1) * LANES)
        q2 = q_ref[:, sl]
        qq = jnp.concatenate([jnp.where(lo, q2, zero), jnp.where(lo, zero, q2)], axis=0)
        kw = jnp.concatenate([r[:, sl] for r in k_refs], axis=0)
        vw = jnp.concatenate([r[:, sl] for r in v_refs], axis=0)
        s = lax.dot_general(qq, kw, NT_DIMS, preferred_element_type=F32) + b_ref[0, sl, :]
        m = jnp.max(s, axis=1, keepdims=True)
        p = jnp.exp(s - m)
        l = jnp.sum(p, axis=1, keepdims=True)
        o = jnp.dot(p.astype(BF16), vw, preferred_element_type=F32) / l
        o_ref[:, sl] = jnp.where(lo, o[:GRID_W], o[GRID_W:]).astype(BF16)


def _na_attention(q, k, v, bias, tok_off, batch, t_len):
    rows = t_len // GRID_W
    assert rows >= NA_KH and tok_off % GRID_W == 0
    roff = tok_off // GRID_W

    def r0(r):
        return jnp.clip(r - NA_KH // 2, 0, rows - NA_KH)

    blk = (GRID_W, D_MODEL)
    q_spec = pl.BlockSpec(blk, lambda b, r: (roff + b * rows + r, 0))
    kv_specs = [pl.BlockSpec(blk, functools.partial(lambda b, r, j: (roff + b * rows + r0(r) + j, 0), j=j))
                for j in range(NA_KH)]
    b_spec = pl.BlockSpec((1,) + bias.shape[1:], lambda b, r: (r0(r) - r + NA_KH - 1, 0, 0))
    return pl.pallas_call(
        _na_attn_kernel,
        grid=(batch, rows),
        in_specs=[q_spec] + kv_specs + kv_specs + [b_spec],
        out_specs=pl.BlockSpec(blk, lambda b, r: (b * rows + r, 0)),
        out_shape=jax.ShapeDtypeStruct((batch * t_len, D_MODEL), BF16),
        compiler_params=_cparams(("arbitrary", "arbitrary")),
        name="na_attn",
    )(q, *([k] * NA_KH), *([v] * NA_KH), bias)


def _mix_post_kernel(x_ref, o_ref, wo_ref, g_ref, wq_ref, x_out, h_out, q_out):
    x = x_ref[...] + jnp.dot(o_ref[...], wo_ref[...], preferred_element_type=F32)
    x_out[...] = x
    h = _rms(x, g_ref[...]).astype(BF16)
    h_out[...] = h
    q = jnp.dot(h, wq_ref[...], preferred_element_type=F32).astype(BF16)
    for c in range(q_out.shape[0]):
        q_out[c] = q[:, c * LANES:(c + 1) * LANES]


def _mix_post(x, o, wo, g, wq, tm):
    n = x.shape[0]
    row = lambda w: pl.BlockSpec((tm, w), lambda i: (i, 0))
    nq = wq.shape[1] // LANES
    return pl.pallas_call(
        _mix_post_kernel,
        grid=(n // tm,),
        in_specs=[row(D_MODEL), row(D_MODEL), _full_spec(wo.shape), _full_spec(g.shape), _full_spec(wq.shape)],
        out_specs=[row(D_MODEL), row(D_MODEL), pl.BlockSpec((nq, tm, LANES), lambda i: (0, i, 0))],
        out_shape=[jax.ShapeDtypeStruct((n, D_MODEL), F32),
                   jax.ShapeDtypeStruct((n, D_MODEL), BF16),
                   jax.ShapeDtypeStruct((nq, n, LANES), BF16)],
        compiler_params=_cparams(("arbitrary",)),
        name="mix_post",
    )(x, o, wo, g, wq)


IDX_NONE = float(1 << 20)


def _top16(s, idx):
    vals, idxs = [], []
    for _ in range(PEER_TOPK):
        m = jnp.max(s, axis=0, keepdims=True)
        first = jnp.min(jnp.where(s == m, idx, IDX_NONE), axis=0, keepdims=True)
        vals.append(m)
        idxs.append(first)
        s = jnp.where(idx == first, -jnp.inf, s)
    return vals, idxs


PEER_NCAND = 2 * PEER_TOPK + 6 * 8


def _cand_flat_index(tm):
    r = lax.broadcasted_iota(jnp.int32, (PEER_NCAND, tm), 0)
    mid = ((r - 16) // 8 + 1) * PEER_TOPK + (r - 16) % 8
    tail = (r - PEER_NCAND + 16) * PEER_TOPK
    return jnp.where(r < 16, r, jnp.where(r < PEER_NCAND - 8, mid, tail)).astype(F32)


def _peer_route_kernel(q_ref, keys_ref, ia_out, ib_out, g_out, ia_scr, ib_scr, g_scr):
    tm = q_ref.shape[1]
    iota_k = lax.broadcasted_iota(jnp.int32, (PEER_NKEYS, tm), 0).astype(F32)
    flat_c = _cand_flat_index(tm)

    def head(h, _):
        halves = []
        for p in range(2):
            qh = q_ref[h * 2 + p]
            st = lax.dot_general(keys_ref[h * 2 + p], qh, NT_DIMS, preferred_element_type=F32)
            halves.append(_top16(st, iota_k))
        (va, ia), (vb, ib) = halves
        va_all = jnp.concatenate(va, axis=0)
        vb_all = jnp.concatenate(vb, axis=0)
        cand = jnp.concatenate([va[0] + vb_all] + [va[i] + vb_all[:8] for i in range(1, 8)]
                               + [va_all[8:] + vb[0]], axis=0)
        fv, fi = _top16(cand, flat_c)
        fv = jnp.concatenate(fv, axis=0)
        fi = jnp.concatenate(fi, axis=0).astype(jnp.int32)
        fa = fi // PEER_TOPK
        fb = fi % PEER_TOPK
        sel_a = jnp.zeros(fv.shape, F32)
        sel_b = jnp.zeros(fv.shape, F32)
        for i in range(PEER_TOPK):
            sel_a = jnp.where(fa == i, ia[i], sel_a)
            sel_b = jnp.where(fb == i, ib[i], sel_b)
        e = jnp.exp(fv - fv[0:1])
        g = e / jnp.sum(e, axis=0, keepdims=True)
        row = pl.multiple_of(h * PEER_TOPK, PEER_TOPK)
        ia_scr[pl.ds(row, PEER_TOPK), :] = sel_a.astype(jnp.int32)
        ib_scr[pl.ds(row, PEER_TOPK), :] = sel_b.astype(jnp.int32)
        g_scr[pl.ds(row, PEER_TOPK), :] = g
        return 0

    lax.fori_loop(0, PEER_HEADS, head, 0)
    ia_out[...] = ia_scr[...].T
    ib_out[...] = ib_scr[...].T
    g_out[...] = g_scr[...].T


def _peer_route(qp, keys, tm):
    n = qp.shape[1]
    row = lambda w: pl.BlockSpec((tm, w), lambda i: (i, 0))
    return pl.pallas_call(
        _peer_route_kernel,
        grid=(n // tm,),
        in_specs=[pl.BlockSpec((qp.shape[0], tm, LANES), lambda i: (0, i, 0)), _full_spec(keys.shape)],
        out_specs=[row(PEER_SEL), row(PEER_SEL), row(PEER_SEL)],
        out_shape=[jax.ShapeDtypeStruct((n, PEER_SEL), jnp.int32),
                   jax.ShapeDtypeStruct((n, PEER_SEL), jnp.int32),
                   jax.ShapeDtypeStruct((n, PEER_SEL), F32)],
        scratch_shapes=[pltpu.VMEM((PEER_SEL, tm), jnp.int32),
                        pltpu.VMEM((PEER_SEL, tm), jnp.int32),
                        pltpu.VMEM((PEER_SEL, tm), F32)],
        compiler_params=_cparams(("arbitrary",)),
        name="peer_route",
    )(qp, keys)


W_PITCH = PEER_NKEYS + 8


def _peer_dense_kernel(h_ref, ia_ref, ib_ref, g_ref, x_ref, u_ref, v_ref, o_ref, w_scr, acc_ref, *, eb):
    j = pl.program_id(1)
    tm = h_ref.shape[0]
    nblk = eb // PEER_NKEYS

    @pl.when(j == 0)
    def _():
        acc_ref[...] = jnp.zeros_like(acc_ref)
        sub = lax.broadcasted_iota(jnp.int32, (PEER_NKEYS, PEER_SEL), 0)

        def build(c8, _):
            base = pl.multiple_of(c8 * 8, 8)
            ia8 = ia_ref[pl.ds(base, 8), :]
            ib8 = ib_ref[pl.ds(base, 8), :]
            g8 = g_ref[pl.ds(base, 8), :]
            for t in range(8):
                pt = jnp.where(sub == ia8[t:t + 1], 1.0, 0.0).astype(BF16)
                qt = jnp.where(sub == ib8[t:t + 1], g8[t:t + 1], 0.0).astype(BF16)
                w = lax.dot_general(pt, qt, NT_DIMS, preferred_element_type=F32)
                row = pl.multiple_of((base + t) * W_PITCH, 8)
                w_scr[pl.ds(row, PEER_NKEYS), :] = w
            return 0

        lax.fori_loop(0, tm // 8, build, 0)

    a = lax.dot_general(h_ref[...], u_ref[...], NT_DIMS, preferred_element_type=F32)
    gl = 0.5 * a * (1.0 + lax.erf(a * (2.0 ** -0.5)))
    parts = []
    for k in range(nblk):
        wk = w_scr[pl.ds(j * nblk + k, tm, stride=W_PITCH), :]
        parts.append((gl[:, k * PEER_NKEYS:(k + 1) * PEER_NKEYS] * wk).astype(BF16))
    wg = jnp.concatenate(parts, axis=1)
    acc_ref[...] += jnp.dot(wg, v_ref[...], preferred_element_type=F32)

    @pl.when(j == pl.num_programs(1) - 1)
    def _():
        o_ref[...] = x_ref[...] + acc_ref[...]


def _peer_dense(h, ia, ib, g, x, u, v, tm, eb):
    n = h.shape[0]
    ne = u.shape[0]
    row = lambda w: pl.BlockSpec((tm, w), lambda i, j: (i, 0))
    return pl.pallas_call(
        functools.partial(_peer_dense_kernel, eb=eb),
        grid=(n // tm, ne // eb),
        in_specs=[row(D_MODEL), row(PEER_SEL), row(PEER_SEL), row(PEER_SEL), row(D_MODEL),
                  pl.BlockSpec((eb, D_MODEL), lambda i, j: (j, 0)),
                  pl.BlockSpec((eb, D_MODEL), lambda i, j: (j, 0))],
        out_specs=row(D_MODEL),
        out_shape=jax.ShapeDtypeStruct((n, D_MODEL), F32),
        scratch_shapes=[pltpu.VMEM((tm * W_PITCH, PEER_NKEYS), F32),
                        pltpu.VMEM((tm, D_MODEL), F32)],
        compiler_params=_cparams(("arbitrary", "arbitrary")),
        name="peer_dense",
    )(h, ia, ib, g, x, u, v)


def _final_norm_kernel(x_ref, g_ref, o_ref):
    o_ref[...] = _rms(x_ref[...], g_ref[...])


def _final_norm(x, g, tm):
    n = x.shape[0]
    row = pl.BlockSpec((tm, D_MODEL), lambda i: (i, 0))
    return pl.pallas_call(
        _final_norm_kernel,
        grid=(n // tm,),
        in_specs=[row, _full_spec(g.shape)],
        out_specs=row,
        out_shape=jax.ShapeDtypeStruct((n, D_MODEL), F32),
        compiler_params=_cparams(("arbitrary",)),
        name="final_norm",
    )(x, g)


def _swap_halves(w):
    half = MLA_ROPE // 2
    return jnp.concatenate([w[..., half:], w[..., :half]], axis=-1)


def _pad_lanes(w):
    return jnp.pad(w, [(0, 0)] * (w.ndim - 1) + [(0, LANES - w.shape[-1])])


def _mla_weights(w_dq, w_uq, w_dkv, w_ukv):
    kpe = w_dkv[:, MLA_KV_LORA:]
    wd = jnp.concatenate([w_dq, w_dkv[:, :MLA_KV_LORA], _pad_lanes(kpe), _pad_lanes(_swap_halves(kpe))], axis=1)
    uq = w_uq.reshape(MLA_Q_LORA, MLA_HEADS, MLA_NOPE + MLA_ROPE)
    pe = uq[:, :, MLA_NOPE:]
    flat = lambda a: a.reshape(MLA_Q_LORA, -1)
    wuq = jnp.concatenate([flat(uq[:, :, :MLA_NOPE]), flat(_pad_lanes(pe)), flat(_pad_lanes(_swap_halves(pe)))], axis=1)
    ukv = w_ukv.reshape(MLA_KV_LORA, MLA_HEADS, MLA_NOPE + MLA_V)
    wukv = jnp.concatenate([ukv[:, :, :MLA_NOPE].reshape(MLA_KV_LORA, -1),
                            ukv[:, :, MLA_NOPE:].reshape(MLA_KV_LORA, -1)], axis=1)
    return wd.astype(BF16), wuq.astype(BF16), wukv.astype(BF16)


def _rope_tables(segments):
    inv = ROPE_THETA ** (-jnp.arange(0, MLA_ROPE, 2, dtype=F32) / MLA_ROPE)
    cs, ss = [], []
    for batch, t_len in segments:
        ang = jnp.arange(t_len, dtype=F32)[:, None] * inv[None, :]
        cos, sin = jnp.cos(ang), jnp.sin(ang)
        cs.append(jnp.tile(_pad_lanes(jnp.concatenate([cos, cos], axis=1)), (batch, 1)))
        ss.append(jnp.tile(_pad_lanes(jnp.concatenate([-sin, sin], axis=1)), (batch, 1)))
    return jnp.concatenate(cs), jnp.concatenate(ss)


def _pick_tile(n, pref):
    t = pref
    while n % t:
        t //= 2
    return t


def _trunk(x_prompt, x_sample, norm_mix, norm_ffn, norm_final,
           mla_w_dq, mla_q_norm, mla_w_uq, mla_w_dkv, mla_kv_norm, mla_w_ukv, mla_w_o,
           na_w_qkv, na_b_qkv, na_rpb, na_w_o, peer_w_q, peer_sub_keys, peer_u, peer_v):
    segs = [(x_prompt.shape[0], x_prompt.shape[1]), (x_sample.shape[0], x_sample.shape[1])]
    offs = [0, segs[0][0] * segs[0][1]]
    x = jnp.concatenate([x_prompt.reshape(-1, D_MODEL), x_sample.reshape(-1, D_MODEL)], axis=0)
    n = x.shape[0]
    tm = _pick_tile(n, 256)
    ctab, stab = _rope_tables(segs)
    row2 = lambda a: a.reshape(1, -1)

    for i in range(DEPTH):
        j = i // 2
        if i % 2 == 0:
            wd, wuq, wukv = _mla_weights(mla_w_dq[j], mla_w_uq[j], mla_w_dkv[j], mla_w_ukv[j])
            q, k, v = _mla_pre(x, row2(norm_mix[i]), wd, row2(mla_q_norm[j]), row2(mla_kv_norm[j]),
                               wuq, wukv, ctab, stab, tm)
            o = jnp.concatenate([
                _mla_attention(q, k, v, off, b, t, _pick_tile(t, 256), _pick_tile(t // 2, 1024))
                for off, (b, t) in zip(offs, segs)], axis=0)
            wo = mla_w_o[j].astype(BF16)
        else:
            q, k, v = _na_pre(x, row2(norm_mix[i]), na_w_qkv[j].astype(BF16), row2(na_b_qkv[j]), tm)
            bias = _na_bias_table(na_rpb[j])
            o = jnp.concatenate([
                _na_attention(q, k, v, bias, off, b, t) for off, (b, t) in zip(offs, segs)], axis=0)
            wo = na_w_o[j].astype(BF16)
        x, h, qp = _mix_post(x, o, wo, row2(norm_ffn[i]), peer_w_q[i].astype(BF16), tm)
        keys = peer_sub_keys[i].reshape(PEER_HEADS * 2, PEER_NKEYS, -1).astype(BF16)
        ia, ib, g = _peer_route(qp, keys, tm)
        x = _peer_dense(h, ia, ib, g, x, peer_u[i].astype(BF16), peer_v[i].astype(BF16), tm, 1024)

    y = _final_norm(x, row2(norm_final), tm)
    return y[:offs[1]].reshape(x_prompt.shape), y[offs[1]:].reshape(x_sample.shape)


def kernel(x_prompt, x_sample, norm_mix, norm_ffn, norm_final, mla_w_dq, mla_q_norm, mla_w_uq, mla_w_dkv, mla_kv_norm, mla_w_ukv, mla_w_o, na_w_qkv, na_b_qkv, na_rpb, na_w_o, peer_w_q, peer_sub_keys, peer_u, peer_v):
    return _trunk(x_prompt, x_sample, norm_mix, norm_ffn, norm_final,
                  mla_w_dq, mla_q_norm, mla_w_uq, mla_w_dkv, mla_kv_norm, mla_w_ukv, mla_w_o,
                  na_w_qkv, na_b_qkv, na_rpb, na_w_o, peer_w_q, peer_sub_keys, peer_u, peer_v)
```

```python
import functools
import math

import jax
import jax.numpy as jnp
from jax import lax
from jax.experimental import pallas as pl
from jax.experimental.pallas import tpu as pltpu

F32 = jnp.float32
BF16 = jnp.bfloat16

D_MODEL = 1024
DEPTH = 4
RMS_EPS = 1e-6
GRID_W = 64
MLA_HEADS = 8
MLA_NOPE = 128
MLA_ROPE = 64
MLA_V = 128
MLA_Q_LORA = 384
MLA_KV_LORA = 256
MLA_QK_PAD = 256
ROPE_THETA = 10000.0
NA_HEADS = 16
NA_HEAD_DIM = 64
NA_KH = 8
NA_KW = 16
NA_RPB_H = 15
NA_RPB_W = 31
PEER_HEADS = 8
PEER_NKEYS = 128
PEER_TOPK = 16
PEER_SEL = PEER_HEADS * PEER_TOPK
NEG_BIG = -1e30

LANES = 128
VMEM_LIMIT = 56 * 1024 * 1024

NT_DIMS = (((1,), (1,)), ((), ()))


def _cparams(sem):
    return pltpu.CompilerParams(dimension_semantics=sem, vmem_limit_bytes=VMEM_LIMIT)


def _rms(x, g):
    return x * lax.rsqrt(jnp.mean(x * x, axis=-1, keepdims=True) + RMS_EPS) * g


def _full_spec(shape):
    n = len(shape)
    return pl.BlockSpec(shape, lambda *_: (0,) * n)


def _mla_pre_kernel(x_ref, g_ref, wd_ref, qn_ref, kvn_ref, wuq_ref, wukv_ref, c_ref, s_ref,
                    q_out, k_out, v_out, *, scale):
    h = _rms(x_ref[...], g_ref[...]).astype(BF16)
    d = jnp.dot(h, wd_ref[...], preferred_element_type=F32)
    cq = _rms(d[:, :MLA_Q_LORA], qn_ref[...]).astype(BF16)
    ckv = _rms(d[:, MLA_Q_LORA:MLA_Q_LORA + MLA_KV_LORA], kvn_ref[...]).astype(BF16)
    c = c_ref[...]
    s = s_ref[...]
    o = MLA_Q_LORA + MLA_KV_LORA
    kpe = (d[:, o:o + LANES] * c + d[:, o + LANES:o + 2 * LANES] * s).astype(BF16)
    q = jnp.dot(cq, wuq_ref[...], preferred_element_type=F32)
    kv = jnp.dot(ckv, wukv_ref[...], preferred_element_type=F32)
    hw = MLA_HEADS * LANES
    for hh in range(MLA_HEADS):
        lo = hh * MLA_QK_PAD
        sl = slice(hh * LANES, (hh + 1) * LANES)
        q_out[:, lo:lo + LANES] = (q[:, sl] * scale).astype(BF16)
        a = q[:, hw + hh * LANES:hw + (hh + 1) * LANES]
        b = q[:, 2 * hw + hh * LANES:2 * hw + (hh + 1) * LANES]
        q_out[:, lo + LANES:lo + 2 * LANES] = ((a * c + b * s) * scale).astype(BF16)
        k_out[:, lo:lo + LANES] = kv[:, sl].astype(BF16)
        k_out[:, lo + LANES:lo + 2 * LANES] = kpe
    v_out[...] = kv[:, hw:].astype(BF16)


def _mla_pre(x, g, wd, qn, kvn, wuq, wukv, ctab, stab, tm):
    n = x.shape[0]
    scale = float((MLA_NOPE + MLA_ROPE) ** -0.5 * math.log2(math.e))
    row = lambda w: pl.BlockSpec((tm, w), lambda i: (i, 0))
    return pl.pallas_call(
        functools.partial(_mla_pre_kernel, scale=scale),
        grid=(n // tm,),
        in_specs=[row(D_MODEL), _full_spec(g.shape), _full_spec(wd.shape), _full_spec(qn.shape),
                  _full_spec(kvn.shape), _full_spec(wuq.shape), _full_spec(wukv.shape),
                  row(LANES), row(LANES)],
        out_specs=[row(MLA_HEADS * MLA_QK_PAD), row(MLA_HEADS * MLA_QK_PAD), row(MLA_HEADS * MLA_V)],
        out_shape=[jax.ShapeDtypeStruct((n, MLA_HEADS * MLA_QK_PAD), BF16),
                   jax.ShapeDtypeStruct((n, MLA_HEADS * MLA_QK_PAD), BF16),
                   jax.ShapeDtypeStruct((n, MLA_HEADS * MLA_V), BF16)],
        compiler_params=_cparams(("arbitrary",)),
        name="mla_pre",
    )(x, g, wd, qn, kvn, wuq, wukv, ctab, stab)


FLASH_MAX_UNROLL = 16


def _flash_kernel(q_ref, k_ref, v_ref, o_ref, sa_ref, sb_ref, *, tk, nk, unroll):
    q = q_ref[...]
    tq = q.shape[0]

    def scores(j, dst):
        start = pl.multiple_of(j * tk, tk)
        dst[...] = lax.dot_general(q, k_ref[pl.ds(start, tk), :], NT_DIMS, preferred_element_type=F32)

    def update(j, src, m, l, acc):
        start = pl.multiple_of(j * tk, tk)
        vs = v_ref[pl.ds(start, tk), :]
        s = src[...]
        m_new = jnp.maximum(m, jnp.max(s, axis=1, keepdims=True))
        alpha = jnp.exp2(m - m_new)
        p = jnp.exp2(s - m_new)
        l = alpha * l + jnp.sum(p, axis=1, keepdims=True)
        acc = alpha * acc + jnp.dot(p.astype(BF16), vs, preferred_element_type=F32)
        return m_new, l, acc

    bufs = (sa_ref, sb_ref)

    def body(jj, carry):
        m, l, acc = carry
        j = unroll * jj
        for u in range(unroll):
            nxt = j + u + 1
            if u == unroll - 1:
                nxt = jnp.minimum(nxt, nk - 1)
            scores(nxt, bufs[(u + 1) % 2])
            m, l, acc = update(j + u, bufs[u % 2], m, l, acc)
        return m, l, acc

    m0 = jnp.full((tq, 1), -jnp.inf, F32)
    l0 = jnp.zeros((tq, 1), F32)
    acc0 = jnp.zeros((tq, MLA_V), F32)
    scores(0, sa_ref)
    _, l, acc = lax.fori_loop(0, nk // unroll, body, (m0, l0, acc0))
    o_ref[...] = (acc / l).astype(BF16)


def _mla_attention(q, k, v, tok_off, batch, t_len, tq, tk):
    assert tok_off % t_len == 0 and t_len % tq == 0 and t_len % tk == 0
    nq = t_len // tq
    nk = t_len // tk
    unroll = math.gcd(nk, FLASH_MAX_UNROLL)
    assert unroll % 2 == 0
    qoff = tok_off // tq
    koff = tok_off // t_len
    return pl.pallas_call(
        functools.partial(_flash_kernel, tk=tk, nk=nk, unroll=unroll),
        grid=(batch, MLA_HEADS, nq),
        scratch_shapes=[pltpu.VMEM((tq, tk), F32), pltpu.VMEM((tq, tk), F32)],
        in_specs=[pl.BlockSpec((tq, MLA_QK_PAD), lambda b, h, i: (qoff + b * nq + i, h)),
                  pl.BlockSpec((t_len, MLA_QK_PAD), lambda b, h, i: (koff + b, h)),
                  pl.BlockSpec((t_len, MLA_V), lambda b, h, i: (koff + b, h))],
        out_specs=pl.BlockSpec((tq, MLA_V), lambda b, h, i: (b * nq + i, h)),
        out_shape=jax.ShapeDtypeStruct((batch * t_len, MLA_HEADS * MLA_V), BF16),
        compiler_params=_cparams(("arbitrary", "arbitrary", "arbitrary")),
        name="mla_flash",
    )(q, k, v)


def _na_pre_kernel(x_ref, g_ref, w_ref, b_ref, q_out, k_out, v_out, *, scale):
    h = _rms(x_ref[...], g_ref[...]).astype(BF16)
    qkv = jnp.dot(h, w_ref[...], preferred_element_type=F32) + b_ref[...]
    q_out[...] = (qkv[:, :D_MODEL] * scale).astype(BF16)
    k_out[...] = qkv[:, D_MODEL:2 * D_MODEL].astype(BF16)
    v_out[...] = qkv[:, 2 * D_MODEL:].astype(BF16)


def _na_pre(x, g, w, b, tm):
    n = x.shape[0]
    row = pl.BlockSpec((tm, D_MODEL), lambda i: (i, 0))
    out = jax.ShapeDtypeStruct((n, D_MODEL), BF16)
    return pl.pallas_call(
        functools.partial(_na_pre_kernel, scale=float(NA_HEAD_DIM ** -0.5)),
        grid=(n // tm,),
        in_specs=[row, _full_spec(g.shape), _full_spec(w.shape), _full_spec(b.shape)],
        out_specs=[row, row, row],
        out_shape=[out, out, out],
        compiler_params=_cparams(("arbitrary",)),
        name="na_pre",
    )(x, g, w, b)


def _na_bias_kernel(rpb_ref, oh_ref, valid_ref, o_ref):
    rpb = rpb_ref[...]
    acc = jnp.zeros(o_ref.shape, F32)
    for d in range(NA_RPB_W):
        acc = jnp.where(oh_ref[d:d + 1, :] > 0, rpb[:, d:d + 1], acc)
    o_ref[...] = jnp.where(valid_ref[...] > 0, acc, NEG_BIG)


def _na_bias_table(rpb):
    c = jnp.arange(GRID_W)[:, None]
    kc = jnp.arange(GRID_W)[None, :]
    rel = (kc - c + NA_KW - 1).reshape(1, GRID_W * GRID_W)
    oh = (rel == jnp.arange(32)[:, None]).astype(F32)
    c0 = jnp.clip(c - NA_KW // 2, 0, GRID_W - NA_KW)
    valid = ((kc >= c0) & (kc < c0 + NA_KW)).astype(F32).reshape(1, GRID_W * GRID_W)
    rpb2 = jnp.pad(rpb.reshape(NA_HEADS * NA_RPB_H, NA_RPB_W), ((0, 0), (0, 1)))
    toep = pl.pallas_call(
        _na_bias_kernel,
        out_shape=jax.ShapeDtypeStruct((NA_HEADS * NA_RPB_H, GRID_W * GRID_W), F32),
        name="na_bias",
    )(rpb2, oh, valid)
    toep = toep.reshape(NA_HEADS, NA_RPB_H, GRID_W, GRID_W)
    tabs = []
    for base in range(NA_KH):
        t = toep[:, base:base + NA_KH]
        tabs.append(t.transpose(0, 2, 1, 3).reshape(NA_HEADS * GRID_W, NA_KH * GRID_W))
    return jnp.stack(tabs)


def _na_attn_kernel(*refs):
    q_ref = refs[0]
    k_refs = refs[1:1 + NA_KH]
    v_refs = refs[1 + NA_KH:1 + 2 * NA_KH]
    b_ref = refs[1 + 2 * NA_KH]
    o_ref = refs[2 + 2 * NA_KH]
    lane = lax.broadcasted_iota(jnp.int32, (GRID_W, LANES), 1)
    lo = lane < NA_HEAD_DIM
    zero = jnp.zeros((GRID_W, LANES), BF16)
    for hp in range(NA_HEADS // 2):
        sl = slice(hp * LANES, (hp + 1) * LANES)
        q2 = q_ref[:, sl]
        qq = jnp.concatenate([jnp.where(lo, q2, zero), jnp.where(lo, zero, q2)], axis=0)
        kw = jnp.concatenate([r[:, sl] for r in k_refs], axis=0)
        vw = jnp.concatenate([r[:, sl] for r in v_refs], axis=0)
        s = lax.dot_general(qq, kw, NT_DIMS, preferred_element_type=F32) + b_ref[0, sl, :]
        m = jnp.max(s, axis=1, keepdims=True)
        p = jnp.exp(s - m)
        l = jnp.sum(p, axis=1, keepdims=True)
        o = jnp.dot(p.astype(BF16), vw, preferred_element_type=F32) / l
        o_ref[:, sl] = jnp.where(lo, o[:GRID_W], o[GRID_W:]).astype(BF16)


def _na_attention(q, k, v, bias, tok_off, batch, t_len):
    rows = t_len // GRID_W
    assert rows >= NA_KH and tok_off % GRID_W == 0
    roff = tok_off // GRID_W

    def r0(r):
        return jnp.clip(r - NA_KH // 2, 0, rows - NA_KH)

    blk = (GRID_W, D_MODEL)
    q_spec = pl.BlockSpec(blk, lambda b, r: (roff + b * rows + r, 0))
    kv_specs = [pl.BlockSpec(blk, functools.partial(lambda b, r, j: (roff + b * rows + r0(r) + j, 0), j=j))
                for j in range(NA_KH)]
    b_spec = pl.BlockSpec((1,) + bias.shape[1:], lambda b, r: (r0(r) - r + NA_KH - 1, 0, 0))
    return pl.pallas_call(
        _na_attn_kernel,
        grid=(batch, rows),
        in_specs=[q_spec] + kv_specs + kv_specs + [b_spec],
        out_specs=pl.BlockSpec(blk, lambda b, r: (b * rows + r, 0)),
        out_shape=jax.ShapeDtypeStruct((batch * t_len, D_MODEL), BF16),
        compiler_params=_cparams(("arbitrary", "arbitrary")),
        name="na_attn",
    )(q, *([k] * NA_KH), *([v] * NA_KH), bias)


def _mix_post_kernel(x_ref, o_ref, wo_ref, g_ref, wq_ref, x_out, h_out, q_out):
    x = x_ref[...] + jnp.dot(o_ref[...], wo_ref[...], preferred_element_type=F32)
    x_out[...] = x
    h = _rms(x, g_ref[...]).astype(BF16)
    h_out[...] = h
    q = jnp.dot(h, wq_ref[...], preferred_element_type=F32).astype(BF16)
    for c in range(q_out.shape[0]):
        q_out[c] = q[:, c * LANES:(c + 1) * LANES]


def _mix_post(x, o, wo, g, wq, tm):
    n = x.shape[0]
    row = lambda w: pl.BlockSpec((tm, w), lambda i: (i, 0))
    nq = wq.shape[1] // LANES
    return pl.pallas_call(
        _mix_post_kernel,
        grid=(n // tm,),
        in_specs=[row(D_MODEL), row(D_MODEL), _full_spec(wo.shape), _full_spec(g.shape), _full_spec(wq.shape)],
        out_specs=[row(D_MODEL), row(D_MODEL), pl.BlockSpec((nq, tm, LANES), lambda i: (0, i, 0))],
        out_shape=[jax.ShapeDtypeStruct((n, D_MODEL), F32),
                   jax.ShapeDtypeStruct((n, D_MODEL), BF16),
                   jax.ShapeDtypeStruct((nq, n, LANES), BF16)],
        compiler_params=_cparams(("arbitrary",)),
        name="mix_post",
    )(x, o, wo, g, wq)


IDX_NONE = float(1 << 20)


def _top16(s, idx):
    vals, idxs = [], []
    for _ in range(PEER_TOPK):
        m = jnp.max(s, axis=0, keepdims=True)
        first = jnp.min(jnp.where(s == m, idx, IDX_NONE), axis=0, keepdims=True)
        vals.append(m)
        idxs.append(first)
        s = jnp.where(idx == first, -jnp.inf, s)
    return vals, idxs


PEER_NCAND = 2 * PEER_TOPK + 6 * 8


def _cand_flat_index(tm):
    r = lax.broadcasted_iota(jnp.int32, (PEER_NCAND, tm), 0)
    mid = ((r - 16) // 8 + 1) * PEER_TOPK + (r - 16) % 8
    tail = (r - PEER_NCAND + 16) * PEER_TOPK
    return jnp.where(r < 16, r, jnp.where(r < PEER_NCAND - 8, mid, tail)).astype(F32)


def _peer_route_kernel(q_ref, keys_ref, ia_out, ib_out, g_out, ia_scr, ib_scr, g_scr):
    tm = q_ref.shape[1]
    iota_k = lax.broadcasted_iota(jnp.int32, (PEER_NKEYS, tm), 0).astype(F32)
    flat_c = _cand_flat_index(tm)

    def head(h, _):
        halves = []
        for p in range(2):
            qh = q_ref[h * 2 + p]
            st = lax.dot_general(keys_ref[h * 2 + p], qh, NT_DIMS, preferred_element_type=F32)
            halves.append(_top16(st, iota_k))
        (va, ia), (vb, ib) = halves
        va_all = jnp.concatenate(va, axis=0)
        vb_all = jnp.concatenate(vb, axis=0)
        cand = jnp.concatenate([va[0] + vb_all] + [va[i] + vb_all[:8] for i in range(1, 8)]
                               + [va_all[8:] + vb[0]], axis=0)
        fv, fi = _top16(cand, flat_c)
        fv = jnp.concatenate(fv, axis=0)
        fi = jnp.concatenate(fi, axis=0).astype(jnp.int32)
        fa = fi // PEER_TOPK
        fb = fi % PEER_TOPK
        sel_a = jnp.zeros(fv.shape, F32)
        sel_b = jnp.zeros(fv.shape, F32)
        for i in range(PEER_TOPK):
            sel_a = jnp.where(fa == i, ia[i], sel_a)
            sel_b = jnp.where(fb == i, ib[i], sel_b)
        e = jnp.exp(fv - fv[0:1])
        g = e / jnp.sum(e, axis=0, keepdims=True)
        row = pl.multiple_of(h * PEER_TOPK, PEER_TOPK)
        ia_scr[pl.ds(row, PEER_TOPK), :] = sel_a.astype(jnp.int32)
        ib_scr[pl.ds(row, PEER_TOPK), :] = sel_b.astype(jnp.int32)
        g_scr[pl.ds(row, PEER_TOPK), :] = g
        return 0

    lax.fori_loop(0, PEER_HEADS, head, 0)
    ia_out[...] = ia_scr[...].T
    ib_out[...] = ib_scr[...].T
    g_out[...] = g_scr[...].T


def _peer_route(qp, keys, tm):
    n = qp.shape[1]
    row = lambda w: pl.BlockSpec((tm, w), lambda i: (i, 0))
    return pl.pallas_call(
        _peer_route_kernel,
        grid=(n // tm,),
        in_specs=[pl.BlockSpec((qp.shape[0], tm, LANES), lambda i: (0, i, 0)), _full_spec(keys.shape)],
        out_specs=[row(PEER_SEL), row(PEER_SEL), row(PEER_SEL)],
        out_shape=[jax.ShapeDtypeStruct((n, PEER_SEL), jnp.int32),
                   jax.ShapeDtypeStruct((n, PEER_SEL), jnp.int32),
                   jax.ShapeDtypeStruct((n, PEER_SEL), F32)],
        scratch_shapes=[pltpu.VMEM((PEER_SEL, tm), jnp.int32),
                        pltpu.VMEM((PEER_SEL, tm), jnp.int32),
                        pltpu.VMEM((PEER_SEL, tm), F32)],
        compiler_params=_cparams(("arbitrary",)),
        name="peer_route",
    )(qp, keys)


W_PITCH = PEER_NKEYS + 8
W_BUILD_UNROLL = 32


def _peer_dense_kernel(h_ref, ia_ref, ib_ref, g_ref, x_ref, u_ref, v_ref, o_ref, w_scr, acc_ref, *, eb):
    j = pl.program_id(1)
    tm = h_ref.shape[0]
    nblk = eb // PEER_NKEYS

    @pl.when(j == 0)
    def _():
        acc_ref[...] = jnp.zeros_like(acc_ref)
        sub = lax.broadcasted_iota(jnp.int32, (PEER_NKEYS, PEER_SEL), 0)

        def build(cb, _):
            base = pl.multiple_of(cb * W_BUILD_UNROLL, W_BUILD_UNROLL)
            ias = ia_ref[pl.ds(base, W_BUILD_UNROLL), :]
            ibs = ib_ref[pl.ds(base, W_BUILD_UNROLL), :]
            gs = g_ref[pl.ds(base, W_BUILD_UNROLL), :] * 0.5
            for t in range(W_BUILD_UNROLL):
                pt = jnp.where(sub == ias[t:t + 1], 1.0, 0.0).astype(BF16)
                qt = jnp.where(sub == ibs[t:t + 1], gs[t:t + 1], 0.0).astype(BF16)
                w = lax.dot_general(pt, qt, NT_DIMS, preferred_element_type=F32)
                row = pl.multiple_of((base + t) * W_PITCH, 8)
                w_scr[pl.ds(row, PEER_NKEYS), :] = w
            return 0

        lax.fori_loop(0, tm // W_BUILD_UNROLL, build, 0)

    a = lax.dot_general(h_ref[...], u_ref[...], NT_DIMS, preferred_element_type=F32)
    parts = []
    for k in range(nblk):
        wk = w_scr[pl.ds(j * nblk + k, tm, stride=W_PITCH), :]
        ak = a[:, k * PEER_NKEYS:(k + 1) * PEER_NKEYS]
        parts.append((ak * (1.0 + lax.erf(ak * (2.0 ** -0.5))) * wk).astype(BF16))
    wg = jnp.concatenate(parts, axis=1)
    acc_ref[...] += jnp.dot(wg, v_ref[...], preferred_element_type=F32)

    @pl.when(j == pl.num_programs(1) - 1)
    def _():
        o_ref[...] = x_ref[...] + acc_ref[...]


def _peer_dense(h, ia, ib, g, x, u, v, tm, eb):
    n = h.shape[0]
    ne = u.shape[0]
    row = lambda w: pl.BlockSpec((tm, w), lambda i, j: (i, 0))
    return pl.pallas_call(
        functools.partial(_peer_dense_kernel, eb=eb),
        grid=(n // tm, ne // eb),
        in_specs=[row(D_MODEL), row(PEER_SEL), row(PEER_SEL), row(PEER_SEL), row(D_MODEL),
                  pl.BlockSpec((eb, D_MODEL), lambda i, j: (j, 0)),
                  pl.BlockSpec((eb, D_MODEL), lambda i, j: (j, 0))],
        out_specs=row(D_MODEL),
        out_shape=jax.ShapeDtypeStruct((n, D_MODEL), F32),
        scratch_shapes=[pltpu.VMEM((tm * W_PITCH, PEER_NKEYS), F32),
                        pltpu.VMEM((tm, D_MODEL), F32)],
        compiler_params=_cparams(("arbitrary", "arbitrary")),
        name="peer_dense",
    )(h, ia, ib, g, x, u, v)


def _final_norm_kernel(x_ref, g_ref, o_ref):
    o_ref[...] = _rms(x_ref[...], g_ref[...])


def _final_norm(x, g, tm):
    n = x.shape[0]
    row = pl.BlockSpec((tm, D_MODEL), lambda i: (i, 0))
    return pl.pallas_call(
        _final_norm_kernel,
        grid=(n // tm,),
        in_specs=[row, _full_spec(g.shape)],
        out_specs=row,
        out_shape=jax.ShapeDtypeStruct((n, D_MODEL), F32),
        compiler_params=_cparams(("arbitrary",)),
        name="final_norm",
    )(x, g)


def _swap_halves(w):
    half = MLA_ROPE // 2
    return jnp.concatenate([w[..., half:], w[..., :half]], axis=-1)


def _pad_lanes(w):
    return jnp.pad(w, [(0, 0)] * (w.ndim - 1) + [(0, LANES - w.shape[-1])])


def _mla_weights(w_dq, w_uq, w_dkv, w_ukv):
    kpe = w_dkv[:, MLA_KV_LORA:]
    wd = jnp.concatenate([w_dq, w_dkv[:, :MLA_KV_LORA], _pad_lanes(kpe), _pad_lanes(_swap_halves(kpe))], axis=1)
    uq = w_uq.reshape(MLA_Q_LORA, MLA_HEADS, MLA_NOPE + MLA_ROPE)
    pe = uq[:, :, MLA_NOPE:]
    flat = lambda a: a.reshape(MLA_Q_LORA, -1)
    wuq = jnp.concatenate([flat(uq[:, :, :MLA_NOPE]), flat(_pad_lanes(pe)), flat(_pad_lanes(_swap_halves(pe)))], axis=1)
    ukv = w_ukv.reshape(MLA_KV_LORA, MLA_HEADS, MLA_NOPE + MLA_V)
    wukv = jnp.concatenate([ukv[:, :, :MLA_NOPE].reshape(MLA_KV_LORA, -1),
                            ukv[:, :, MLA_NOPE:].reshape(MLA_KV_LORA, -1)], axis=1)
    return wd.astype(BF16), wuq.astype(BF16), wukv.astype(BF16)


def _rope_tables(segments):
    inv = ROPE_THETA ** (-jnp.arange(0, MLA_ROPE, 2, dtype=F32) / MLA_ROPE)
    cs, ss = [], []
    for batch, t_len in segments:
        ang = jnp.arange(t_len, dtype=F32)[:, None] * inv[None, :]
        cos, sin = jnp.cos(ang), jnp.sin(ang)
        cs.append(jnp.tile(_pad_lanes(jnp.concatenate([cos, cos], axis=1)), (batch, 1)))
        ss.append(jnp.tile(_pad_lanes(jnp.concatenate([-sin, sin], axis=1)), (batch, 1)))
    return jnp.concatenate(cs), jnp.concatenate(ss)


def _pick_tile(n, pref):
    t = pref
    while n % t:
        t //= 2
    return t


def _trunk(x_prompt, x_sample, norm_mix, norm_ffn, norm_final,
           mla_w_dq, mla_q_norm, mla_w_uq, mla_w_dkv, mla_kv_norm, mla_w_ukv, mla_w_o,
           na_w_qkv, na_b_qkv, na_rpb, na_w_o, peer_w_q, peer_sub_keys, peer_u, peer_v):
    segs = [(x_prompt.shape[0], x_prompt.shape[1]), (x_sample.shape[0], x_sample.shape[1])]
    offs = [0, segs[0][0] * segs[0][1]]
    x = jnp.concatenate([x_prompt.reshape(-1, D_MODEL), x_sample.reshape(-1, D_MODEL)], axis=0)
    n = x.shape[0]
    tm = _pick_tile(n, 256)
    ctab, stab = _rope_tables(segs)
    row2 = lambda a: a.reshape(1, -1)

    for i in range(DEPTH):
        j = i // 2
        if i % 2 == 0:
            wd, wuq, wukv = _mla_weights(mla_w_dq[j], mla_w_uq[j], mla_w_dkv[j], mla_w_ukv[j])
            q, k, v = _mla_pre(x, row2(norm_mix[i]), wd, row2(mla_q_norm[j]), row2(mla_kv_norm[j]),
                               wuq, wukv, ctab, stab, tm)
            o = jnp.concatenate([
                _mla_attention(q, k, v, off, b, t, _pick_tile(t, 256), _pick_tile(t // 2, 1024))
                for off, (b, t) in zip(offs, segs)], axis=0)
            wo = mla_w_o[j].astype(BF16)
        else:
            q, k, v = _na_pre(x, row2(norm_mix[i]), na_w_qkv[j].astype(BF16), row2(na_b_qkv[j]), tm)
            bias = _na_bias_table(na_rpb[j])
            o = jnp.concatenate([
                _na_attention(q, k, v, bias, off, b, t) for off, (b, t) in zip(offs, segs)], axis=0)
            wo = na_w_o[j].astype(BF16)
        x, h, qp = _mix_post(x, o, wo, row2(norm_ffn[i]), peer_w_q[i].astype(BF16), tm)
        keys = peer_sub_keys[i].reshape(PEER_HEADS * 2, PEER_NKEYS, -1).astype(BF16)
        ia, ib, g = _peer_route(qp, keys, _pick_tile(n, 512))
        x = _peer_dense(h, ia, ib, g, x, peer_u[i].astype(BF16), peer_v[i].astype(BF16), tm, 2048)

    y = _final_norm(x, row2(norm_final), tm)
    return y[:offs[1]].reshape(x_prompt.shape), y[offs[1]:].reshape(x_sample.shape)


def kernel(x_prompt, x_sample, norm_mix, norm_ffn, norm_final, mla_w_dq, mla_q_norm, mla_w_uq, mla_w_dkv, mla_kv_norm, mla_w_ukv, mla_w_o, na_w_qkv, na_b_qkv, na_rpb, na_w_o, peer_w_q, peer_sub_keys, peer_u, peer_v):
    return _trunk(x_prompt, x_sample, norm_mix, norm_ffn, norm_final,
                  mla_w_dq, mla_q_norm, mla_w_uq, mla_w_dkv, mla_kv_norm, mla_w_ukv, mla_w_o,
                  na_w_qkv, na_b_qkv, na_rpb, na_w_o, peer_w_q, peer_sub_keys, peer_u, peer_v)
```

```python
import functools
import math

import jax
import jax.numpy as jnp
from jax import lax
from jax.experimental import pallas as pl
from jax.experimental.pallas import tpu as pltpu

F32 = jnp.float32
BF16 = jnp.bfloat16

D_MODEL = 1024
DEPTH = 4
RMS_EPS = 1e-6
GRID_W = 64
MLA_HEADS = 8
MLA_NOPE = 128
MLA_ROPE = 64
MLA_V = 128
MLA_Q_LORA = 384
MLA_KV_LORA = 256
MLA_QK_PAD = 256
ROPE_THETA = 10000.0
NA_HEADS = 16
NA_HEAD_DIM = 64
NA_KH = 8
NA_KW = 16
NA_RPB_H = 15
NA_RPB_W = 31
PEER_HEADS = 8
PEER_NKEYS = 128
PEER_TOPK = 16
PEER_SEL = PEER_HEADS * PEER_TOPK
NEG_BIG = -1e30

LANES = 128
VMEM_LIMIT = 56 * 1024 * 1024

NT_DIMS = (((1,), (1,)), ((), ()))


def _cparams(sem):
    return pltpu.CompilerParams(dimension_semantics=sem, vmem_limit_bytes=VMEM_LIMIT)


def _rms(x, g):
    return x * lax.rsqrt(jnp.mean(x * x, axis=-1, keepdims=True) + RMS_EPS) * g


def _full_spec(shape):
    n = len(shape)
    return pl.BlockSpec(shape, lambda *_: (0,) * n)


def _mla_pre_kernel(x_ref, g_ref, wd_ref, qn_ref, kvn_ref, wuq_ref, wukv_ref, c_ref, s_ref,
                    q_out, k_out, v_out, *, scale):
    h = _rms(x_ref[...], g_ref[...]).astype(BF16)
    d = jnp.dot(h, wd_ref[...], preferred_element_type=F32)
    cq = _rms(d[:, :MLA_Q_LORA], qn_ref[...]).astype(BF16)
    ckv = _rms(d[:, MLA_Q_LORA:MLA_Q_LORA + MLA_KV_LORA], kvn_ref[...]).astype(BF16)
    c = c_ref[...]
    s = s_ref[...]
    o = MLA_Q_LORA + MLA_KV_LORA
    kpe = (d[:, o:o + LANES] * c + d[:, o + LANES:o + 2 * LANES] * s).astype(BF16)
    q = jnp.dot(cq, wuq_ref[...], preferred_element_type=F32)
    kv = jnp.dot(ckv, wukv_ref[...], preferred_element_type=F32)
    hw = MLA_HEADS * LANES
    for hh in range(MLA_HEADS):
        lo = hh * MLA_QK_PAD
        sl = slice(hh * LANES, (hh + 1) * LANES)
        q_out[:, lo:lo + LANES] = (q[:, sl] * scale).astype(BF16)
        a = q[:, hw + hh * LANES:hw + (hh + 1) * LANES]
        b = q[:, 2 * hw + hh * LANES:2 * hw + (hh + 1) * LANES]
        q_out[:, lo + LANES:lo + 2 * LANES] = ((a * c + b * s) * scale).astype(BF16)
        k_out[:, lo:lo + LANES] = kv[:, sl].astype(BF16)
        k_out[:, lo + LANES:lo + 2 * LANES] = kpe
    v_out[...] = kv[:, hw:].astype(BF16)


def _mla_pre(x, g, wd, qn, kvn, wuq, wukv, ctab, stab, tm):
    n = x.shape[0]
    scale = float((MLA_NOPE + MLA_ROPE) ** -0.5 * math.log2(math.e))
    row = lambda w: pl.BlockSpec((tm, w), lambda i: (i, 0))
    return pl.pallas_call(
        functools.partial(_mla_pre_kernel, scale=scale),
        grid=(n // tm,),
        in_specs=[row(D_MODEL), _full_spec(g.shape), _full_spec(wd.shape), _full_spec(qn.shape),
                  _full_spec(kvn.shape), _full_spec(wuq.shape), _full_spec(wukv.shape),
                  row(LANES), row(LANES)],
        out_specs=[row(MLA_HEADS * MLA_QK_PAD), row(MLA_HEADS * MLA_QK_PAD), row(MLA_HEADS * MLA_V)],
        out_shape=[jax.ShapeDtypeStruct((n, MLA_HEADS * MLA_QK_PAD), BF16),
                   jax.ShapeDtypeStruct((n, MLA_HEADS * MLA_QK_PAD), BF16),
                   jax.ShapeDtypeStruct((n, MLA_HEADS * MLA_V), BF16)],
        compiler_params=_cparams(("arbitrary",)),
        name="mla_pre",
    )(x, g, wd, qn, kvn, wuq, wukv, ctab, stab)


FLASH_MAX_UNROLL = 16


def _flash_kernel(q_ref, k_ref, v_ref, o_ref, sa_ref, sb_ref, *, tk, nk, unroll):
    q = q_ref[...]
    tq = q.shape[0]

    def scores(j, dst):
        start = pl.multiple_of(j * tk, tk)
        dst[...] = lax.dot_general(q, k_ref[pl.ds(start, tk), :], NT_DIMS, preferred_element_type=F32)

    def update(j, src, m, l, acc):
        start = pl.multiple_of(j * tk, tk)
        vs = v_ref[pl.ds(start, tk), :]
        s = src[...]
        m_new = jnp.maximum(m, jnp.max(s, axis=1, keepdims=True))
        alpha = jnp.exp2(m - m_new)
        p = jnp.exp2(s - m_new)
        l = alpha * l + jnp.sum(p, axis=1, keepdims=True)
        acc = alpha * acc + jnp.dot(p.astype(BF16), vs, preferred_element_type=F32)
        return m_new, l, acc

    bufs = (sa_ref, sb_ref)

    def body(jj, carry):
        m, l, acc = carry
        j = unroll * jj
        for u in range(unroll):
            nxt = j + u + 1
            if u == unroll - 1:
                nxt = jnp.minimum(nxt, nk - 1)
            scores(nxt, bufs[(u + 1) % 2])
            m, l, acc = update(j + u, bufs[u % 2], m, l, acc)
        return m, l, acc

    m0 = jnp.full((tq, 1), -jnp.inf, F32)
    l0 = jnp.zeros((tq, 1), F32)
    acc0 = jnp.zeros((tq, MLA_V), F32)
    scores(0, sa_ref)
    _, l, acc = lax.fori_loop(0, nk // unroll, body, (m0, l0, acc0))
    o_ref[...] = (acc / l).astype(BF16)


def _mla_attention(q, k, v, tok_off, batch, t_len, tq, tk):
    assert tok_off % t_len == 0 and t_len % tq == 0 and t_len % tk == 0
    nq = t_len // tq
    nk = t_len // tk
    unroll = math.gcd(nk, FLASH_MAX_UNROLL)
    assert unroll % 2 == 0
    qoff = tok_off // tq
    koff = tok_off // t_len
    return pl.pallas_call(
        functools.partial(_flash_kernel, tk=tk, nk=nk, unroll=unroll),
        grid=(batch, MLA_HEADS, nq),
        scratch_shapes=[pltpu.VMEM((tq, tk), F32), pltpu.VMEM((tq, tk), F32)],
        in_specs=[pl.BlockSpec((tq, MLA_QK_PAD), lambda b, h, i: (qoff + b * nq + i, h)),
                  pl.BlockSpec((t_len, MLA_QK_PAD), lambda b, h, i: (koff + b, h)),
                  pl.BlockSpec((t_len, MLA_V), lambda b, h, i: (koff + b, h))],
        out_specs=pl.BlockSpec((tq, MLA_V), lambda b, h, i: (b * nq + i, h)),
        out_shape=jax.ShapeDtypeStruct((batch * t_len, MLA_HEADS * MLA_V), BF16),
        compiler_params=_cparams(("arbitrary", "arbitrary", "arbitrary")),
        name="mla_flash",
    )(q, k, v)


def _na_pre_kernel(x_ref, g_ref, w_ref, b_ref, q_out, kv_out, *, scale):
    h = _rms(x_ref[...], g_ref[...]).astype(BF16)
    qkv = jnp.dot(h, w_ref[...], preferred_element_type=F32) + b_ref[...]
    q_out[...] = (qkv[:, :D_MODEL] * scale).astype(BF16)
    kv_out[...] = qkv[:, D_MODEL:].astype(BF16)


def _na_pre(x, g, w, b, tm):
    n = x.shape[0]
    row = lambda c: pl.BlockSpec((tm, c), lambda i: (i, 0))
    return pl.pallas_call(
        functools.partial(_na_pre_kernel, scale=float(NA_HEAD_DIM ** -0.5)),
        grid=(n // tm,),
        in_specs=[row(D_MODEL), _full_spec(g.shape), _full_spec(w.shape), _full_spec(b.shape)],
        out_specs=[row(D_MODEL), row(2 * D_MODEL)],
        out_shape=[jax.ShapeDtypeStruct((n, D_MODEL), BF16), jax.ShapeDtypeStruct((n, 2 * D_MODEL), BF16)],
        compiler_params=_cparams(("arbitrary",)),
        name="na_pre",
    )(x, g, w, b)


def _na_bias_kernel(rpb_ref, oh_ref, valid_ref, o_ref):
    rpb = rpb_ref[...]
    acc = jnp.zeros(o_ref.shape, F32)
    for d in range(NA_RPB_W):
        acc = jnp.where(oh_ref[d:d + 1, :] > 0, rpb[:, d:d + 1], acc)
    o_ref[...] = jnp.where(valid_ref[...] > 0, acc, NEG_BIG)


def _na_bias_table(rpb):
    c = jnp.arange(GRID_W)[:, None]
    kc = jnp.arange(GRID_W)[None, :]
    rel = (kc - c + NA_KW - 1).reshape(1, GRID_W * GRID_W)
    oh = (rel == jnp.arange(32)[:, None]).astype(F32)
    c0 = jnp.clip(c - NA_KW // 2, 0, GRID_W - NA_KW)
    valid = ((kc >= c0) & (kc < c0 + NA_KW)).astype(F32).reshape(1, GRID_W * GRID_W)
    rpb2 = jnp.pad(rpb.reshape(NA_HEADS * NA_RPB_H, NA_RPB_W), ((0, 0), (0, 1)))
    toep = pl.pallas_call(
        _na_bias_kernel,
        out_shape=jax.ShapeDtypeStruct((NA_HEADS * NA_RPB_H, GRID_W * GRID_W), F32),
        name="na_bias",
    )(rpb2, oh, valid)
    toep = toep.reshape(NA_HEADS, NA_RPB_H, GRID_W, GRID_W)
    tabs = []
    for base in range(NA_KH):
        t = toep[:, base:base + NA_KH]
        tabs.append(t.transpose(0, 2, 1, 3).reshape(NA_HEADS * GRID_W, NA_KH * GRID_W))
    return jnp.stack(tabs)


def _na_attn_kernel(*refs):
    q_ref = refs[0]
    kv_refs = refs[1:1 + NA_KH]
    b_ref = refs[1 + NA_KH]
    o_ref = refs[2 + NA_KH]
    lane = lax.broadcasted_iota(jnp.int32, (GRID_W, LANES), 1)
    lo = lane < NA_HEAD_DIM
    zero = jnp.zeros((GRID_W, LANES), BF16)
    pairs = range(NA_HEADS // 2)
    scores = []
    for hp in pairs:
        sl = slice(hp * LANES, (hp + 1) * LANES)
        q2 = q_ref[:, sl]
        qq = jnp.concatenate([jnp.where(lo, q2, zero), jnp.where(lo, zero, q2)], axis=0)
        kw = jnp.concatenate([r[:, sl] for r in kv_refs], axis=0)
        scores.append(lax.dot_general(qq, kw, NT_DIMS, preferred_element_type=F32) + b_ref[0, sl, :])
    probs, sums = [], []
    for s in scores:
        p = jnp.exp(s - jnp.max(s, axis=1, keepdims=True))
        sums.append(jnp.sum(p, axis=1, keepdims=True))
        probs.append(p.astype(BF16))
    for hp in pairs:
        sl = slice(hp * LANES, (hp + 1) * LANES)
        vw = jnp.concatenate([r[:, D_MODEL + hp * LANES:D_MODEL + (hp + 1) * LANES] for r in kv_refs], axis=0)
        o = jnp.dot(probs[hp], vw, preferred_element_type=F32) / sums[hp]
        o_ref[:, sl] = jnp.where(lo, o[:GRID_W], o[GRID_W:]).astype(BF16)


def _na_attention(q, kv, bias, tok_off, batch, t_len):
    rows = t_len // GRID_W
    assert rows >= NA_KH and tok_off % GRID_W == 0
    roff = tok_off // GRID_W

    def r0(r):
        return jnp.clip(r - NA_KH // 2, 0, rows - NA_KH)

    blk = (GRID_W, D_MODEL)
    q_spec = pl.BlockSpec(blk, lambda b, r: (roff + b * rows + r, 0))
    kv_specs = [pl.BlockSpec((GRID_W, 2 * D_MODEL),
                             functools.partial(lambda b, r, j: (roff + b * rows + r0(r) + j, 0), j=j))
                for j in range(NA_KH)]
    b_spec = pl.BlockSpec((1,) + bias.shape[1:], lambda b, r: (r0(r) - r + NA_KH - 1, 0, 0))
    return pl.pallas_call(
        _na_attn_kernel,
        grid=(batch, rows),
        in_specs=[q_spec] + kv_specs + [b_spec],
        out_specs=pl.BlockSpec(blk, lambda b, r: (b * rows + r, 0)),
        out_shape=jax.ShapeDtypeStruct((batch * t_len, D_MODEL), BF16),
        compiler_params=_cparams(("arbitrary", "arbitrary")),
        name="na_attn",
    )(q, *([kv] * NA_KH), bias)


def _mix_post_kernel(x_ref, o_ref, wo_ref, g_ref, wq_ref, x_out, h_out, q_out):
    x = x_ref[...] + jnp.dot(o_ref[...], wo_ref[...], preferred_element_type=F32)
    x_out[...] = x
    h = _rms(x, g_ref[...]).astype(BF16)
    h_out[...] = h
    q = jnp.dot(h, wq_ref[...], preferred_element_type=F32).astype(BF16)
    for c in range(q_out.shape[0]):
        q_out[c] = q[:, c * LANES:(c + 1) * LANES]


def _mix_post(x, o, wo, g, wq, tm):
    n = x.shape[0]
    row = lambda w: pl.BlockSpec((tm, w), lambda i: (i, 0))
    nq = wq.shape[1] // LANES
    return pl.pallas_call(
        _mix_post_kernel,
        grid=(n // tm,),
        in_specs=[row(D_MODEL), row(D_MODEL), _full_spec(wo.shape), _full_spec(g.shape), _full_spec(wq.shape)],
        out_specs=[row(D_MODEL), row(D_MODEL), pl.BlockSpec((nq, tm, LANES), lambda i: (0, i, 0))],
        out_shape=[jax.ShapeDtypeStruct((n, D_MODEL), F32),
                   jax.ShapeDtypeStruct((n, D_MODEL), BF16),
                   jax.ShapeDtypeStruct((nq, n, LANES), BF16)],
        compiler_params=_cparams(("arbitrary",)),
        name="mix_post",
    )(x, o, wo, g, wq)


IDX_NONE = float(1 << 20)


def _top16(s, idx):
    vals, idxs = [], []
    for _ in range(PEER_TOPK):
        m = jnp.max(s, axis=0, keepdims=True)
        first = jnp.min(jnp.where(s == m, idx, IDX_NONE), axis=0, keepdims=True)
        vals.append(m)
        idxs.append(first)
        s = jnp.where(idx == first, -jnp.inf, s)
    return vals, idxs


PEER_NCAND = 2 * PEER_TOPK + 6 * 8


def _cand_flat_index(tm):
    r = lax.broadcasted_iota(jnp.int32, (PEER_NCAND, tm), 0)
    mid = ((r - 16) // 8 + 1) * PEER_TOPK + (r - 16) % 8
    tail = (r - PEER_NCAND + 16) * PEER_TOPK
    return jnp.where(r < 16, r, jnp.where(r < PEER_NCAND - 8, mid, tail)).astype(F32)


def _peer_route_kernel(q_ref, keys_ref, ia_out, ib_out, g_out, ia_scr, ib_scr, g_scr):
    tm = q_ref.shape[1]
    iota_k = lax.broadcasted_iota(jnp.int32, (PEER_NKEYS, tm), 0).astype(F32)
    flat_c = _cand_flat_index(tm)

    def head(h, _):
        halves = []
        for p in range(2):
            qh = q_ref[h * 2 + p]
            st = lax.dot_general(keys_ref[h * 2 + p], qh, NT_DIMS, preferred_element_type=F32)
            halves.append(_top16(st, iota_k))
        (va, ia), (vb, ib) = halves
        va_all = jnp.concatenate(va, axis=0)
        vb_all = jnp.concatenate(vb, axis=0)
        cand = jnp.concatenate([va[0] + vb_all] + [va[i] + vb_all[:8] for i in range(1, 8)]
                               + [va_all[8:] + vb[0]], axis=0)
        fv, fi = _top16(cand, flat_c)
        fv = jnp.concatenate(fv, axis=0)
        fi = jnp.concatenate(fi, axis=0).astype(jnp.int32)
        fa = fi // PEER_TOPK
        fb = fi % PEER_TOPK
        sel_a = jnp.zeros(fv.shape, F32)
        sel_b = jnp.zeros(fv.shape, F32)
        for i in range(PEER_TOPK):
            sel_a = jnp.where(fa == i, ia[i], sel_a)
            sel_b = jnp.where(fb == i, ib[i], sel_b)
        e = jnp.exp(fv - fv[0:1])
        g = e / jnp.sum(e, axis=0, keepdims=True)
        row = pl.multiple_of(h * PEER_TOPK, PEER_TOPK)
        ia_scr[pl.ds(row, PEER_TOPK), :] = sel_a.astype(jnp.int32)
        ib_scr[pl.ds(row, PEER_TOPK), :] = sel_b.astype(jnp.int32)
        g_scr[pl.ds(row, PEER_TOPK), :] = g
        return 0

    lax.fori_loop(0, PEER_HEADS, head, 0)
    ia_out[...] = ia_scr[...].T
    ib_out[...] = ib_scr[...].T
    g_out[...] = g_scr[...].T


def _peer_route(qp, keys, tm):
    n = qp.shape[1]
    row = lambda w: pl.BlockSpec((tm, w), lambda i: (i, 0))
    return pl.pallas_call(
        _peer_route_kernel,
        grid=(n // tm,),
        in_specs=[pl.BlockSpec((qp.shape[0], tm, LANES), lambda i: (0, i, 0)), _full_spec(keys.shape)],
        out_specs=[row(PEER_SEL), row(PEER_SEL), row(PEER_SEL)],
        out_shape=[jax.ShapeDtypeStruct((n, PEER_SEL), jnp.int32),
                   jax.ShapeDtypeStruct((n, PEER_SEL), jnp.int32),
                   jax.ShapeDtypeStruct((n, PEER_SEL), F32)],
        scratch_shapes=[pltpu.VMEM((PEER_SEL, tm), jnp.int32),
                        pltpu.VMEM((PEER_SEL, tm), jnp.int32),
                        pltpu.VMEM((PEER_SEL, tm), F32)],
        compiler_params=_cparams(("arbitrary",)),
        name="peer_route",
    )(qp, keys)


W_PITCH = PEER_NKEYS + 8
W_BUILD_UNROLL = 16
HI_MASK = 0xFFFF0000
VMEM_LIMIT_DENSE = 60 * 1024 * 1024


def _gate_tile_bits(sub, ia, ib, g):
    pt = jnp.where(sub == ia, 1.0, 0.0).astype(BF16)
    qt = jnp.where(sub == ib, g, 0.0).astype(BF16)
    w = lax.dot_general(pt, qt, NT_DIMS, preferred_element_type=F32)
    return lax.bitcast_convert_type(w, jnp.uint32) + jnp.uint32(0x8000)


def _peer_dense_kernel(h_ref, ia_ref, ib_ref, g_ref, x_ref, u_ref, v_ref, o_ref, w_scr, acc_ref, *, eb):
    j = pl.program_id(1)
    tm = h_ref.shape[0]
    half = tm // 2
    nblk = eb // PEER_NKEYS

    @pl.when(j == 0)
    def _():
        acc_ref[...] = jnp.zeros_like(acc_ref)
        sub = lax.broadcasted_iota(jnp.int32, (PEER_NKEYS, PEER_SEL), 0)

        def build(cb, _):
            base = pl.multiple_of(cb * W_BUILD_UNROLL, W_BUILD_UNROLL)
            top = [r[pl.ds(base, W_BUILD_UNROLL), :] for r in (ia_ref, ib_ref, g_ref)]
            bot = [r[pl.ds(base + half, W_BUILD_UNROLL), :] for r in (ia_ref, ib_ref, g_ref)]
            for t in range(W_BUILD_UNROLL):
                row1 = slice(t, t + 1)
                w_top = _gate_tile_bits(sub, top[0][row1], top[1][row1], top[2][row1] * 0.5)
                w_bot = _gate_tile_bits(sub, bot[0][row1], bot[1][row1], bot[2][row1] * 0.5)
                row = pl.multiple_of((base + t) * W_PITCH, 8)
                w_scr[pl.ds(row, PEER_NKEYS), :] = (w_top & jnp.uint32(HI_MASK)) | (w_bot >> 16)
            return 0

        lax.fori_loop(0, half // W_BUILD_UNROLL, build, 0)

    a = lax.dot_general(h_ref[...], u_ref[...], NT_DIMS, preferred_element_type=F32)
    parts = []
    for k in range(nblk):
        bits = w_scr[pl.ds(j * nblk + k, half, stride=W_PITCH), :]
        wk = jnp.concatenate([lax.bitcast_convert_type(bits & jnp.uint32(HI_MASK), F32),
                              lax.bitcast_convert_type(bits << 16, F32)], axis=0)
        ak = a[:, k * PEER_NKEYS:(k + 1) * PEER_NKEYS]
        parts.append((ak * (1.0 + lax.erf(ak * (2.0 ** -0.5))) * wk).astype(BF16))
    wg = jnp.concatenate(parts, axis=1)
    acc_ref[...] += jnp.dot(wg, v_ref[...], preferred_element_type=F32)

    @pl.when(j == pl.num_programs(1) - 1)
    def _():
        o_ref[...] = x_ref[...] + acc_ref[...]


def _peer_dense(h, ia, ib, g, x, u, v, tm, eb):
    n = h.shape[0]
    ne = u.shape[0]
    assert tm % (2 * W_BUILD_UNROLL) == 0
    once = lambda w: pl.BlockSpec((tm, w), lambda i, j: (i, 0), pipeline_mode=pl.Buffered(1))
    return pl.pallas_call(
        functools.partial(_peer_dense_kernel, eb=eb),
        grid=(n // tm, ne // eb),
        in_specs=[once(D_MODEL), once(PEER_SEL), once(PEER_SEL), once(PEER_SEL), once(D_MODEL),
                  pl.BlockSpec((eb, D_MODEL), lambda i, j: (j, 0)),
                  pl.BlockSpec((eb, D_MODEL), lambda i, j: (j, 0))],
        out_specs=pl.BlockSpec((tm, D_MODEL), lambda i, j: (i, 0)),
        out_shape=jax.ShapeDtypeStruct((n, D_MODEL), F32),
        scratch_shapes=[pltpu.VMEM((tm // 2 * W_PITCH, PEER_NKEYS), jnp.uint32),
                        pltpu.VMEM((tm, D_MODEL), F32)],
        compiler_params=pltpu.CompilerParams(dimension_semantics=("arbitrary", "arbitrary"),
                                             vmem_limit_bytes=VMEM_LIMIT_DENSE),
        name="peer_dense",
    )(h, ia, ib, g, x, u, v)


def _final_norm_kernel(x_ref, g_ref, o_ref):
    o_ref[...] = _rms(x_ref[...], g_ref[...])


def _final_norm(x, g, tm):
    n = x.shape[0]
    row = pl.BlockSpec((tm, D_MODEL), lambda i: (i, 0))
    return pl.pallas_call(
        _final_norm_kernel,
        grid=(n // tm,),
        in_specs=[row, _full_spec(g.shape)],
        out_specs=row,
        out_shape=jax.ShapeDtypeStruct((n, D_MODEL), F32),
        compiler_params=_cparams(("arbitrary",)),
        name="final_norm",
    )(x, g)


def _swap_halves(w):
    half = MLA_ROPE // 2
    return jnp.concatenate([w[..., half:], w[..., :half]], axis=-1)


def _pad_lanes(w):
    return jnp.pad(w, [(0, 0)] * (w.ndim - 1) + [(0, LANES - w.shape[-1])])


def _mla_weights(w_dq, w_uq, w_dkv, w_ukv):
    kpe = w_dkv[:, MLA_KV_LORA:]
    wd = jnp.concatenate([w_dq, w_dkv[:, :MLA_KV_LORA], _pad_lanes(kpe), _pad_lanes(_swap_halves(kpe))], axis=1)
    uq = w_uq.reshape(MLA_Q_LORA, MLA_HEADS, MLA_NOPE + MLA_ROPE)
    pe = uq[:, :, MLA_NOPE:]
    flat = lambda a: a.reshape(MLA_Q_LORA, -1)
    wuq = jnp.concatenate([flat(uq[:, :, :MLA_NOPE]), flat(_pad_lanes(pe)), flat(_pad_lanes(_swap_halves(pe)))], axis=1)
    ukv = w_ukv.reshape(MLA_KV_LORA, MLA_HEADS, MLA_NOPE + MLA_V)
    wukv = jnp.concatenate([ukv[:, :, :MLA_NOPE].reshape(MLA_KV_LORA, -1),
                            ukv[:, :, MLA_NOPE:].reshape(MLA_KV_LORA, -1)], axis=1)
    return wd.astype(BF16), wuq.astype(BF16), wukv.astype(BF16)


def _rope_tables(segments):
    inv = ROPE_THETA ** (-jnp.arange(0, MLA_ROPE, 2, dtype=F32) / MLA_ROPE)
    cs, ss = [], []
    for batch, t_len in segments:
        ang = jnp.arange(t_len, dtype=F32)[:, None] * inv[None, :]
        cos, sin = jnp.cos(ang), jnp.sin(ang)
        cs.append(jnp.tile(_pad_lanes(jnp.concatenate([cos, cos], axis=1)), (batch, 1)))
        ss.append(jnp.tile(_pad_lanes(jnp.concatenate([-sin, sin], axis=1)), (batch, 1)))
    return jnp.concatenate(cs), jnp.concatenate(ss)


def _pick_tile(n, pref):
    t = pref
    while n % t:
        t //= 2
    return t


def _trunk(x_prompt, x_sample, norm_mix, norm_ffn, norm_final,
           mla_w_dq, mla_q_norm, mla_w_uq, mla_w_dkv, mla_kv_norm, mla_w_ukv, mla_w_o,
           na_w_qkv, na_b_qkv, na_rpb, na_w_o, peer_w_q, peer_sub_keys, peer_u, peer_v):
    segs = [(x_prompt.shape[0], x_prompt.shape[1]), (x_sample.shape[0], x_sample.shape[1])]
    offs = [0, segs[0][0] * segs[0][1]]
    x = jnp.concatenate([x_prompt.reshape(-1, D_MODEL), x_sample.reshape(-1, D_MODEL)], axis=0)
    n = x.shape[0]
    tm = _pick_tile(n, 256)
    ctab, stab = _rope_tables(segs)
    row2 = lambda a: a.reshape(1, -1)

    for i in range(DEPTH):
        j = i // 2
        if i % 2 == 0:
            wd, wuq, wukv = _mla_weights(mla_w_dq[j], mla_w_uq[j], mla_w_dkv[j], mla_w_ukv[j])
            q, k, v = _mla_pre(x, row2(norm_mix[i]), wd, row2(mla_q_norm[j]), row2(mla_kv_norm[j]),
                               wuq, wukv, ctab, stab, tm)
            o = jnp.concatenate([
                _mla_attention(q, k, v, off, b, t, _pick_tile(t, 256), _pick_tile(t // 2, 1024))
                for off, (b, t) in zip(offs, segs)], axis=0)
            wo = mla_w_o[j].astype(BF16)
        else:
            q, kv = _na_pre(x, row2(norm_mix[i]), na_w_qkv[j].astype(BF16), row2(na_b_qkv[j]), tm)
            bias = _na_bias_table(na_rpb[j])
            o = jnp.concatenate([
                _na_attention(q, kv, bias, off, b, t) for off, (b, t) in zip(offs, segs)], axis=0)
            wo = na_w_o[j].astype(BF16)
        x, h, qp = _mix_post(x, o, wo, row2(norm_ffn[i]), peer_w_q[i].astype(BF16), tm)
        keys = peer_sub_keys[i].reshape(PEER_HEADS * 2, PEER_NKEYS, -1).astype(BF16)
        ia, ib, g = _peer_route(qp, keys, _pick_tile(n, 512))
        x = _peer_dense(h, ia, ib, g, x, peer_u[i].astype(BF16), peer_v[i].astype(BF16),
                        _pick_tile(n, 512), 2048)

    y = _final_norm(x, row2(norm_final), tm)
    return y[:offs[1]].reshape(x_prompt.shape), y[offs[1]:].reshape(x_sample.shape)


def kernel(x_prompt, x_sample, norm_mix, norm_ffn, norm_final, mla_w_dq, mla_q_norm, mla_w_uq, mla_w_dkv, mla_kv_norm, mla_w_ukv, mla_w_o, na_w_qkv, na_b_qkv, na_rpb, na_w_o, peer_w_q, peer_sub_keys, peer_u, peer_v):
    return _trunk(x_prompt, x_sample, norm_mix, norm_ffn, norm_final,
                  mla_w_dq, mla_q_norm, mla_w_uq, mla_w_dkv, mla_kv_norm, mla_w_ukv, mla_w_o,
                  na_w_qkv, na_b_qkv, na_rpb, na_w_o, peer_w_q, peer_sub_keys, peer_u, peer_v)
```

```python
import functools
import math

import jax
import jax.numpy as jnp
from jax import lax
from jax.experimental import pallas as pl
from jax.experimental.pallas import tpu as pltpu

F32 = jnp.float32
BF16 = jnp.bfloat16

D_MODEL = 1024
DEPTH = 4
RMS_EPS = 1e-6
GRID_W = 64
MLA_HEADS = 8
MLA_NOPE = 128
MLA_ROPE = 64
MLA_V = 128
MLA_Q_LORA = 384
MLA_KV_LORA = 256
MLA_QK_PAD = 256
ROPE_THETA = 10000.0
NA_HEADS = 16
NA_HEAD_DIM = 64
NA_KH = 8
NA_KW = 16
NA_RPB_H = 15
NA_RPB_W = 31
PEER_HEADS = 8
PEER_NKEYS = 128
PEER_TOPK = 16
PEER_SEL = PEER_HEADS * PEER_TOPK
NEG_BIG = -1e30

LANES = 128
VMEM_LIMIT = 56 * 1024 * 1024

NT_DIMS = (((1,), (1,)), ((), ()))


def _cparams(sem):
    return pltpu.CompilerParams(dimension_semantics=sem, vmem_limit_bytes=VMEM_LIMIT)


def _rms(x, g):
    return x * lax.rsqrt(jnp.mean(x * x, axis=-1, keepdims=True) + RMS_EPS) * g


def _full_spec(shape):
    n = len(shape)
    return pl.BlockSpec(shape, lambda *_: (0,) * n)


def _mla_pre_kernel(x_ref, g_ref, wd_ref, qn_ref, kvn_ref, wuq_ref, wukv_ref, c_ref, s_ref,
                    q_out, k_out, v_out, *, scale):
    h = _rms(x_ref[...], g_ref[...]).astype(BF16)
    d = jnp.dot(h, wd_ref[...], preferred_element_type=F32)
    cq = _rms(d[:, :MLA_Q_LORA], qn_ref[...]).astype(BF16)
    ckv = _rms(d[:, MLA_Q_LORA:MLA_Q_LORA + MLA_KV_LORA], kvn_ref[...]).astype(BF16)
    c = c_ref[...]
    s = s_ref[...]
    o = MLA_Q_LORA + MLA_KV_LORA
    kpe = (d[:, o:o + LANES] * c + d[:, o + LANES:o + 2 * LANES] * s).astype(BF16)
    q = jnp.dot(cq, wuq_ref[...], preferred_element_type=F32)
    kv = jnp.dot(ckv, wukv_ref[...], preferred_element_type=F32)
    hw = MLA_HEADS * LANES
    for hh in range(MLA_HEADS):
        lo = hh * MLA_QK_PAD
        sl = slice(hh * LANES, (hh + 1) * LANES)
        q_out[:, lo:lo + LANES] = (q[:, sl] * scale).astype(BF16)
        a = q[:, hw + hh * LANES:hw + (hh + 1) * LANES]
        b = q[:, 2 * hw + hh * LANES:2 * hw + (hh + 1) * LANES]
        q_out[:, lo + LANES:lo + 2 * LANES] = ((a * c + b * s) * scale).astype(BF16)
        k_out[:, lo:lo + LANES] = kv[:, sl].astype(BF16)
        k_out[:, lo + LANES:lo + 2 * LANES] = kpe
    v_out[...] = kv[:, hw:].astype(BF16)


def _mla_pre(x, g, wd, qn, kvn, wuq, wukv, ctab, stab, tm):
    n = x.shape[0]
    scale = float((MLA_NOPE + MLA_ROPE) ** -0.5 * math.log2(math.e))
    row = lambda w: pl.BlockSpec((tm, w), lambda i: (i, 0))
    return pl.pallas_call(
        functools.partial(_mla_pre_kernel, scale=scale),
        grid=(n // tm,),
        in_specs=[row(D_MODEL), _full_spec(g.shape), _full_spec(wd.shape), _full_spec(qn.shape),
                  _full_spec(kvn.shape), _full_spec(wuq.shape), _full_spec(wukv.shape),
                  row(LANES), row(LANES)],
        out_specs=[row(MLA_HEADS * MLA_QK_PAD), row(MLA_HEADS * MLA_QK_PAD), row(MLA_HEADS * MLA_V)],
        out_shape=[jax.ShapeDtypeStruct((n, MLA_HEADS * MLA_QK_PAD), BF16),
                   jax.ShapeDtypeStruct((n, MLA_HEADS * MLA_QK_PAD), BF16),
                   jax.ShapeDtypeStruct((n, MLA_HEADS * MLA_V), BF16)],
        compiler_params=_cparams(("arbitrary",)),
        name="mla_pre",
    )(x, g, wd, qn, kvn, wuq, wukv, ctab, stab)


FLASH_MAX_UNROLL = 16


def _flash_kernel(q_ref, k_ref, v_ref, o_ref, sa_ref, sb_ref, *, tk, nk, unroll):
    q = q_ref[...]
    tq = q.shape[0]

    def scores(j, dst):
        start = pl.multiple_of(j * tk, tk)
        dst[...] = lax.dot_general(q, k_ref[pl.ds(start, tk), :], NT_DIMS, preferred_element_type=F32)

    def update(j, src, m, l, acc):
        start = pl.multiple_of(j * tk, tk)
        vs = v_ref[pl.ds(start, tk), :]
        s = src[...]
        m_new = jnp.maximum(m, jnp.max(s, axis=1, keepdims=True))
        alpha = jnp.exp2(m - m_new)
        p = jnp.exp2(s - m_new)
        l = alpha * l + jnp.sum(p, axis=1, keepdims=True)
        acc = alpha * acc + jnp.dot(p.astype(BF16), vs, preferred_element_type=F32)
        return m_new, l, acc

    bufs = (sa_ref, sb_ref)

    def body(jj, carry):
        m, l, acc = carry
        j = unroll * jj
        for u in range(unroll):
            nxt = j + u + 1
            if u == unroll - 1:
                nxt = jnp.minimum(nxt, nk - 1)
            scores(nxt, bufs[(u + 1) % 2])
            m, l, acc = update(j + u, bufs[u % 2], m, l, acc)
        return m, l, acc

    m0 = jnp.full((tq, 1), -jnp.inf, F32)
    l0 = jnp.zeros((tq, 1), F32)
    acc0 = jnp.zeros((tq, MLA_V), F32)
    scores(0, sa_ref)
    _, l, acc = lax.fori_loop(0, nk // unroll, body, (m0, l0, acc0))
    o_ref[...] = (acc / l).astype(BF16)


def _mla_attention(q, k, v, tok_off, batch, t_len, tq, tk):
    assert tok_off % t_len == 0 and t_len % tq == 0 and t_len % tk == 0
    nq = t_len // tq
    nk = t_len // tk
    unroll = math.gcd(nk, FLASH_MAX_UNROLL)
    assert unroll % 2 == 0
    qoff = tok_off // tq
    koff = tok_off // t_len
    return pl.pallas_call(
        functools.partial(_flash_kernel, tk=tk, nk=nk, unroll=unroll),
        grid=(batch, MLA_HEADS, nq),
        scratch_shapes=[pltpu.VMEM((tq, tk), F32), pltpu.VMEM((tq, tk), F32)],
        in_specs=[pl.BlockSpec((tq, MLA_QK_PAD), lambda b, h, i: (qoff + b * nq + i, h)),
                  pl.BlockSpec((t_len, MLA_QK_PAD), lambda b, h, i: (koff + b, h)),
                  pl.BlockSpec((t_len, MLA_V), lambda b, h, i: (koff + b, h))],
        out_specs=pl.BlockSpec((tq, MLA_V), lambda b, h, i: (b * nq + i, h)),
        out_shape=jax.ShapeDtypeStruct((batch * t_len, MLA_HEADS * MLA_V), BF16),
        compiler_params=_cparams(("arbitrary", "arbitrary", "arbitrary")),
        name="mla_flash",
    )(q, k, v)


def _na_pre_kernel(x_ref, g_ref, w_ref, b_ref, q_out, kv_out, *, scale):
    h = _rms(x_ref[...], g_ref[...]).astype(BF16)
    qkv = jnp.dot(h, w_ref[...], preferred_element_type=F32) + b_ref[...]
    q_out[...] = (qkv[:, :D_MODEL] * scale).astype(BF16)
    kv_out[...] = qkv[:, D_MODEL:].astype(BF16)


def _na_pre(x, g, w, b, tm):
    n = x.shape[0]
    row = lambda c: pl.BlockSpec((tm, c), lambda i: (i, 0))
    return pl.pallas_call(
        functools.partial(_na_pre_kernel, scale=float(NA_HEAD_DIM ** -0.5)),
        grid=(n // tm,),
        in_specs=[row(D_MODEL), _full_spec(g.shape), _full_spec(w.shape), _full_spec(b.shape)],
        out_specs=[row(D_MODEL), row(2 * D_MODEL)],
        out_shape=[jax.ShapeDtypeStruct((n, D_MODEL), BF16), jax.ShapeDtypeStruct((n, 2 * D_MODEL), BF16)],
        compiler_params=_cparams(("arbitrary",)),
        name="na_pre",
    )(x, g, w, b)


def _na_bias_kernel(rpb_ref, oh_ref, valid_ref, o_ref):
    rpb = rpb_ref[...]
    acc = jnp.zeros(o_ref.shape, F32)
    for d in range(NA_RPB_W):
        acc = jnp.where(oh_ref[d:d + 1, :] > 0, rpb[:, d:d + 1], acc)
    o_ref[...] = jnp.where(valid_ref[...] > 0, acc, NEG_BIG)


def _na_bias_table(rpb):
    c = jnp.arange(GRID_W)[:, None]
    kc = jnp.arange(GRID_W)[None, :]
    rel = (kc - c + NA_KW - 1).reshape(1, GRID_W * GRID_W)
    oh = (rel == jnp.arange(32)[:, None]).astype(F32)
    c0 = jnp.clip(c - NA_KW // 2, 0, GRID_W - NA_KW)
    valid = ((kc >= c0) & (kc < c0 + NA_KW)).astype(F32).reshape(1, GRID_W * GRID_W)
    rpb2 = jnp.pad(rpb.reshape(NA_HEADS * NA_RPB_H, NA_RPB_W), ((0, 0), (0, 1)))
    toep = pl.pallas_call(
        _na_bias_kernel,
        out_shape=jax.ShapeDtypeStruct((NA_HEADS * NA_RPB_H, GRID_W * GRID_W), F32),
        name="na_bias",
    )(rpb2, oh, valid)
    toep = toep.reshape(NA_HEADS, NA_RPB_H, GRID_W, GRID_W)
    tabs = []
    for base in range(NA_KH):
        t = toep[:, base:base + NA_KH]
        tabs.append(t.transpose(0, 2, 1, 3).reshape(NA_HEADS * GRID_W, NA_KH * GRID_W))
    return jnp.stack(tabs)


def _na_attn_kernel(*refs):
    q_ref = refs[0]
    kv_refs = refs[1:1 + NA_KH]
    b_ref = refs[1 + NA_KH]
    o_ref = refs[2 + NA_KH]
    lane = lax.broadcasted_iota(jnp.int32, (GRID_W, LANES), 1)
    lo = lane < NA_HEAD_DIM
    zero = jnp.zeros((GRID_W, LANES), BF16)
    pairs = range(NA_HEADS // 2)
    scores = []
    for hp in pairs:
        sl = slice(hp * LANES, (hp + 1) * LANES)
        q2 = q_ref[:, sl]
        qq = jnp.concatenate([jnp.where(lo, q2, zero), jnp.where(lo, zero, q2)], axis=0)
        kw = jnp.concatenate([r[:, sl] for r in kv_refs], axis=0)
        scores.append(lax.dot_general(qq, kw, NT_DIMS, preferred_element_type=F32) + b_ref[0, sl, :])
    probs, sums = [], []
    for s in scores:
        p = jnp.exp(s - jnp.max(s, axis=1, keepdims=True))
        sums.append(jnp.sum(p, axis=1, keepdims=True))
        probs.append(p.astype(BF16))
    for hp in pairs:
        sl = slice(hp * LANES, (hp + 1) * LANES)
        vw = jnp.concatenate([r[:, D_MODEL + hp * LANES:D_MODEL + (hp + 1) * LANES] for r in kv_refs], axis=0)
        o = jnp.dot(probs[hp], vw, preferred_element_type=F32) / sums[hp]
        o_ref[:, sl] = jnp.where(lo, o[:GRID_W], o[GRID_W:]).astype(BF16)


def _na_attention(q, kv, bias, tok_off, batch, t_len):
    rows = t_len // GRID_W
    assert rows >= NA_KH and tok_off % GRID_W == 0
    roff = tok_off // GRID_W

    def r0(r):
        return jnp.clip(r - NA_KH // 2, 0, rows - NA_KH)

    blk = (GRID_W, D_MODEL)
    q_spec = pl.BlockSpec(blk, lambda b, r: (roff + b * rows + r, 0))
    kv_specs = [pl.BlockSpec((GRID_W, 2 * D_MODEL),
                             functools.partial(lambda b, r, j: (roff + b * rows + r0(r) + j, 0), j=j))
                for j in range(NA_KH)]
    b_spec = pl.BlockSpec((1,) + bias.shape[1:], lambda b, r: (r0(r) - r + NA_KH - 1, 0, 0))
    return pl.pallas_call(
        _na_attn_kernel,
        grid=(batch, rows),
        in_specs=[q_spec] + kv_specs + [b_spec],
        out_specs=pl.BlockSpec(blk, lambda b, r: (b * rows + r, 0)),
        out_shape=jax.ShapeDtypeStruct((batch * t_len, D_MODEL), BF16),
        compiler_params=_cparams(("arbitrary", "arbitrary")),
        name="na_attn",
    )(q, *([kv] * NA_KH), bias)


def _mix_post_kernel(x_ref, o_ref, wo_ref, g_ref, wq_ref, x_out, h_out, q_out):
    x = x_ref[...] + jnp.dot(o_ref[...], wo_ref[...], preferred_element_type=F32)
    x_out[...] = x
    h = _rms(x, g_ref[...]).astype(BF16)
    h_out[...] = h
    q = jnp.dot(h, wq_ref[...], preferred_element_type=F32).astype(BF16)
    for c in range(q_out.shape[0]):
        q_out[c] = q[:, c * LANES:(c + 1) * LANES]


def _mix_post(x, o, wo, g, wq, tm):
    n = x.shape[0]
    row = lambda w: pl.BlockSpec((tm, w), lambda i: (i, 0))
    nq = wq.shape[1] // LANES
    return pl.pallas_call(
        _mix_post_kernel,
        grid=(n // tm,),
        in_specs=[row(D_MODEL), row(D_MODEL), _full_spec(wo.shape), _full_spec(g.shape), _full_spec(wq.shape)],
        out_specs=[row(D_MODEL), row(D_MODEL), pl.BlockSpec((nq, tm, LANES), lambda i: (0, i, 0))],
        out_shape=[jax.ShapeDtypeStruct((n, D_MODEL), F32),
                   jax.ShapeDtypeStruct((n, D_MODEL), BF16),
                   jax.ShapeDtypeStruct((nq, n, LANES), BF16)],
        compiler_params=_cparams(("arbitrary",)),
        name="mix_post",
    )(x, o, wo, g, wq)


IDX_NONE = float(1 << 20)
SUBLANES = 8


def _top16(s, rowid):
    ngrp = s.shape[0] // SUBLANES
    grp = [s[g * SUBLANES:(g + 1) * SUBLANES] for g in range(ngrp)]
    vals, poss = [], []
    for _ in range(PEER_TOPK):
        m8 = grp[0]
        for g in range(1, ngrp):
            m8 = jnp.maximum(m8, grp[g])
        m = jnp.max(m8, axis=0, keepdims=True)
        t = jnp.full(m8.shape, IDX_NONE, F32)
        for g in reversed(range(ngrp)):
            t = jnp.where(grp[g] == m, float(g * SUBLANES), t)
        first = jnp.min(t + rowid[0], axis=0, keepdims=True)
        vals.append(m)
        poss.append(first)
        grp = [jnp.where(rowid[g] == first, -jnp.inf, grp[g]) for g in range(ngrp)]
    return vals, poss


PEER_NCAND = 2 * PEER_TOPK + 6 * 8


def _cand_flat_index(pos):
    g = pos // SUBLANES
    sb = pos % SUBLANES
    return jnp.where(g < 2, pos, jnp.where(g < PEER_NCAND // SUBLANES - 1,
                                           (g - 1) * PEER_TOPK + sb, (SUBLANES + sb) * PEER_TOPK))


def _peer_route_kernel(q_ref, keys_ref, ia_out, ib_out, g_out, ia_scr, ib_scr, g_scr):
    tm = q_ref.shape[1]
    sub = lax.broadcasted_iota(jnp.int32, (SUBLANES, tm), 0).astype(F32)
    rowid = [sub + float(g * SUBLANES) for g in range(PEER_NKEYS // SUBLANES)]

    def head(h, _):
        halves = []
        for p in range(2):
            qh = q_ref[h * 2 + p]
            st = lax.dot_general(keys_ref[h * 2 + p], qh, NT_DIMS, preferred_element_type=F32)
            halves.append(_top16(st, rowid))
        (va, ia), (vb, ib) = halves
        va_all = jnp.concatenate(va, axis=0)
        vb_all = jnp.concatenate(vb, axis=0)
        cand = jnp.concatenate([va[0] + vb_all] + [va[i] + vb_all[:8] for i in range(1, 8)]
                               + [va_all[8:] + vb[0]], axis=0)
        fv, fp = _top16(cand, rowid)
        fv = jnp.concatenate(fv, axis=0)
        fi = _cand_flat_index(jnp.concatenate(fp, axis=0).astype(jnp.int32))
        fa = fi // PEER_TOPK
        fb = fi % PEER_TOPK
        sel_a = jnp.zeros(fv.shape, F32)
        sel_b = jnp.zeros(fv.shape, F32)
        for i in range(PEER_TOPK):
            sel_a = jnp.where(fa == i, ia[i], sel_a)
            sel_b = jnp.where(fb == i, ib[i], sel_b)
        e = jnp.exp(fv - fv[0:1])
        g = e / jnp.sum(e, axis=0, keepdims=True)
        row = pl.multiple_of(h * PEER_TOPK, PEER_TOPK)
        ia_scr[pl.ds(row, PEER_TOPK), :] = sel_a.astype(jnp.int32)
        ib_scr[pl.ds(row, PEER_TOPK), :] = sel_b.astype(jnp.int32)
        g_scr[pl.ds(row, PEER_TOPK), :] = g
        return 0

    lax.fori_loop(0, PEER_HEADS, head, 0)
    ia_out[...] = ia_scr[...].T
    ib_out[...] = ib_scr[...].T
    g_out[...] = g_scr[...].T


def _peer_route(qp, keys, tm):
    n = qp.shape[1]
    row = lambda w: pl.BlockSpec((tm, w), lambda i: (i, 0))
    return pl.pallas_call(
        _peer_route_kernel,
        grid=(n // tm,),
        in_specs=[pl.BlockSpec((qp.shape[0], tm, LANES), lambda i: (0, i, 0)), _full_spec(keys.shape)],
        out_specs=[row(PEER_SEL), row(PEER_SEL), row(PEER_SEL)],
        out_shape=[jax.ShapeDtypeStruct((n, PEER_SEL), jnp.int32),
                   jax.ShapeDtypeStruct((n, PEER_SEL), jnp.int32),
                   jax.ShapeDtypeStruct((n, PEER_SEL), F32)],
        scratch_shapes=[pltpu.VMEM((PEER_SEL, tm), jnp.int32),
                        pltpu.VMEM((PEER_SEL, tm), jnp.int32),
                        pltpu.VMEM((PEER_SEL, tm), F32)],
        compiler_params=_cparams(("arbitrary",)),
        name="peer_route",
    )(qp, keys)


W_PITCH = PEER_NKEYS + 8
W_BUILD_UNROLL = 16
HI_MASK = 0xFFFF0000
VMEM_LIMIT_DENSE = 60 * 1024 * 1024


def _gate_tile_bits(sub, ia, ib, g):
    pt = jnp.where(sub == ia, 1.0, 0.0).astype(BF16)
    qt = jnp.where(sub == ib, g, 0.0).astype(BF16)
    w = lax.dot_general(pt, qt, NT_DIMS, preferred_element_type=F32)
    return lax.bitcast_convert_type(w, jnp.uint32) + jnp.uint32(0x8000)


def _peer_dense_kernel(h_ref, ia_ref, ib_ref, g_ref, x_ref, u_ref, v_ref, o_ref, w_scr, acc_ref, *, eb):
    j = pl.program_id(1)
    tm = h_ref.shape[0]
    half = tm // 2
    nblk = eb // PEER_NKEYS

    @pl.when(j == 0)
    def _():
        acc_ref[...] = jnp.zeros_like(acc_ref)
        sub = lax.broadcasted_iota(jnp.int32, (PEER_NKEYS, PEER_SEL), 0)

        def build(cb, _):
            base = pl.multiple_of(cb * W_BUILD_UNROLL, W_BUILD_UNROLL)
            top = [r[pl.ds(base, W_BUILD_UNROLL), :] for r in (ia_ref, ib_ref, g_ref)]
            bot = [r[pl.ds(base + half, W_BUILD_UNROLL), :] for r in (ia_ref, ib_ref, g_ref)]
            for t in range(W_BUILD_UNROLL):
                row1 = slice(t, t + 1)
                w_top = _gate_tile_bits(sub, top[0][row1], top[1][row1], top[2][row1] * 0.5)
                w_bot = _gate_tile_bits(sub, bot[0][row1], bot[1][row1], bot[2][row1] * 0.5)
                row = pl.multiple_of((base + t) * W_PITCH, 8)
                w_scr[pl.ds(row, PEER_NKEYS), :] = (w_top & jnp.uint32(HI_MASK)) | (w_bot >> 16)
            return 0

        lax.fori_loop(0, half // W_BUILD_UNROLL, build, 0)

    a = lax.dot_general(h_ref[...], u_ref[...], NT_DIMS, preferred_element_type=F32)
    parts = []
    for k in range(nblk):
        bits = w_scr[pl.ds(j * nblk + k, half, stride=W_PITCH), :]
        wk = jnp.concatenate([lax.bitcast_convert_type(bits & jnp.uint32(HI_MASK), F32),
                              lax.bitcast_convert_type(bits << 16, F32)], axis=0)
        ak = a[:, k * PEER_NKEYS:(k + 1) * PEER_NKEYS]
        parts.append((ak * (1.0 + lax.erf(ak * (2.0 ** -0.5))) * wk).astype(BF16))
    wg = jnp.concatenate(parts, axis=1)
    acc_ref[...] += jnp.dot(wg, v_ref[...], preferred_element_type=F32)

    @pl.when(j == pl.num_programs(1) - 1)
    def _():
        o_ref[...] = x_ref[...] + acc_ref[...]


def _peer_dense(h, ia, ib, g, x, u, v, tm, eb):
    n = h.shape[0]
    ne = u.shape[0]
    assert tm % (2 * W_BUILD_UNROLL) == 0
    row = lambda w: pl.BlockSpec((tm, w), lambda i, j: (i, 0))
    x_spec = pl.BlockSpec((tm, D_MODEL), lambda i, j: (i, 0), pipeline_mode=pl.Buffered(1))
    return pl.pallas_call(
        functools.partial(_peer_dense_kernel, eb=eb),
        grid=(n // tm, ne // eb),
        in_specs=[row(D_MODEL), row(PEER_SEL), row(PEER_SEL), row(PEER_SEL), x_spec,
                  pl.BlockSpec((eb, D_MODEL), lambda i, j: (j, 0)),
                  pl.BlockSpec((eb, D_MODEL), lambda i, j: (j, 0))],
        out_specs=row(D_MODEL),
        out_shape=jax.ShapeDtypeStruct((n, D_MODEL), F32),
        scratch_shapes=[pltpu.VMEM((tm // 2 * W_PITCH, PEER_NKEYS), jnp.uint32),
                        pltpu.VMEM((tm, D_MODEL), F32)],
        compiler_params=pltpu.CompilerParams(dimension_semantics=("arbitrary", "arbitrary"),
                                             vmem_limit_bytes=VMEM_LIMIT_DENSE),
        name="peer_dense",
    )(h, ia, ib, g, x, u, v)


def _final_norm_kernel(x_ref, g_ref, o_ref):
    o_ref[...] = _rms(x_ref[...], g_ref[...])


def _final_norm(x, g, tm):
    n = x.shape[0]
    row = pl.BlockSpec((tm, D_MODEL), lambda i: (i, 0))
    return pl.pallas_call(
        _final_norm_kernel,
        grid=(n // tm,),
        in_specs=[row, _full_spec(g.shape)],
        out_specs=row,
        out_shape=jax.ShapeDtypeStruct((n, D_MODEL), F32),
        compiler_params=_cparams(("arbitrary",)),
        name="final_norm",
    )(x, g)


def _swap_halves(w):
    half = MLA_ROPE // 2
    return jnp.concatenate([w[..., half:], w[..., :half]], axis=-1)


def _pad_lanes(w):
    return jnp.pad(w, [(0, 0)] * (w.ndim - 1) + [(0, LANES - w.shape[-1])])


def _mla_weights(w_dq, w_uq, w_dkv, w_ukv):
    kpe = w_dkv[:, MLA_KV_LORA:]
    wd = jnp.concatenate([w_dq, w_dkv[:, :MLA_KV_LORA], _pad_lanes(kpe), _pad_lanes(_swap_halves(kpe))], axis=1)
    uq = w_uq.reshape(MLA_Q_LORA, MLA_HEADS, MLA_NOPE + MLA_ROPE)
    pe = uq[:, :, MLA_NOPE:]
    flat = lambda a: a.reshape(MLA_Q_LORA, -1)
    wuq = jnp.concatenate([flat(uq[:, :, :MLA_NOPE]), flat(_pad_lanes(pe)), flat(_pad_lanes(_swap_halves(pe)))], axis=1)
    ukv = w_ukv.reshape(MLA_KV_LORA, MLA_HEADS, MLA_NOPE + MLA_V)
    wukv = jnp.concatenate([ukv[:, :, :MLA_NOPE].reshape(MLA_KV_LORA, -1),
                            ukv[:, :, MLA_NOPE:].reshape(MLA_KV_LORA, -1)], axis=1)
    return wd.astype(BF16), wuq.astype(BF16), wukv.astype(BF16)


def _rope_tables(segments):
    inv = ROPE_THETA ** (-jnp.arange(0, MLA_ROPE, 2, dtype=F32) / MLA_ROPE)
    cs, ss = [], []
    for batch, t_len in segments:
        ang = jnp.arange(t_len, dtype=F32)[:, None] * inv[None, :]
        cos, sin = jnp.cos(ang), jnp.sin(ang)
        cs.append(jnp.tile(_pad_lanes(jnp.concatenate([cos, cos], axis=1)), (batch, 1)))
        ss.append(jnp.tile(_pad_lanes(jnp.concatenate([-sin, sin], axis=1)), (batch, 1)))
    return jnp.concatenate(cs), jnp.concatenate(ss)


def _pick_tile(n, pref):
    t = pref
    while n % t:
        t //= 2
    return t


def _trunk(x_prompt, x_sample, norm_mix, norm_ffn, norm_final,
           mla_w_dq, mla_q_norm, mla_w_uq, mla_w_dkv, mla_kv_norm, mla_w_ukv, mla_w_o,
           na_w_qkv, na_b_qkv, na_rpb, na_w_o, peer_w_q, peer_sub_keys, peer_u, peer_v):
    segs = [(x_prompt.shape[0], x_prompt.shape[1]), (x_sample.shape[0], x_sample.shape[1])]
    offs = [0, segs[0][0] * segs[0][1]]
    x = jnp.concatenate([x_prompt.reshape(-1, D_MODEL), x_sample.reshape(-1, D_MODEL)], axis=0)
    n = x.shape[0]
    tm = _pick_tile(n, 256)
    ctab, stab = _rope_tables(segs)
    row2 = lambda a: a.reshape(1, -1)

    for i in range(DEPTH):
        j = i // 2
        if i % 2 == 0:
            wd, wuq, wukv = _mla_weights(mla_w_dq[j], mla_w_uq[j], mla_w_dkv[j], mla_w_ukv[j])
            q, k, v = _mla_pre(x, row2(norm_mix[i]), wd, row2(mla_q_norm[j]), row2(mla_kv_norm[j]),
                               wuq, wukv, ctab, stab, tm)
            o = jnp.concatenate([
                _mla_attention(q, k, v, off, b, t, _pick_tile(t, 256), _pick_tile(t // 2, 1024))
                for off, (b, t) in zip(offs, segs)], axis=0)
            wo = mla_w_o[j].astype(BF16)
        else:
            q, kv = _na_pre(x, row2(norm_mix[i]), na_w_qkv[j].astype(BF16), row2(na_b_qkv[j]), tm)
            bias = _na_bias_table(na_rpb[j])
            o = jnp.concatenate([
                _na_attention(q, kv, bias, off, b, t) for off, (b, t) in zip(offs, segs)], axis=0)
            wo = na_w_o[j].astype(BF16)
        x, h, qp = _mix_post(x, o, wo, row2(norm_ffn[i]), peer_w_q[i].astype(BF16), tm)
        keys = peer_sub_keys[i].reshape(PEER_HEADS * 2, PEER_NKEYS, -1).astype(BF16)
        ia, ib, g = _peer_route(qp, keys, _pick_tile(n, 1024))
        x = _peer_dense(h, ia, ib, g, x, peer_u[i].astype(BF16), peer_v[i].astype(BF16),
                        _pick_tile(n, 512), 2048)

    y = _final_norm(x, row2(norm_final), tm)
    return y[:offs[1]].reshape(x_prompt.shape), y[offs[1]:].reshape(x_sample.shape)


def kernel(x_prompt, x_sample, norm_mix, norm_ffn, norm_final, mla_w_dq, mla_q_norm, mla_w_uq, mla_w_dkv, mla_kv_norm, mla_w_ukv, mla_w_o, na_w_qkv, na_b_qkv, na_rpb, na_w_o, peer_w_q, peer_sub_keys, peer_u, peer_v):
    return _trunk(x_prompt, x_sample, norm_mix, norm_ffn, norm_final,
                  mla_w_dq, mla_q_norm, mla_w_uq, mla_w_dkv, mla_kv_norm, mla_w_ukv, mla_w_o,
                  na_w_qkv, na_b_qkv, na_rpb, na_w_o, peer_w_q, peer_sub_keys, peer_u, peer_v)
```

```python
import functools
import math

import jax
import jax.numpy as jnp
from jax import lax
from jax.experimental import pallas as pl
from jax.experimental.pallas import tpu as pltpu

F32 = jnp.float32
BF16 = jnp.bfloat16

D_MODEL = 1024
DEPTH = 4
RMS_EPS = 1e-6
GRID_W = 64
MLA_HEADS = 8
MLA_NOPE = 128
MLA_ROPE = 64
MLA_V = 128
MLA_Q_LORA = 384
MLA_KV_LORA = 256
MLA_QK_PAD = 256
ROPE_THETA = 10000.0
NA_HEADS = 16
NA_HEAD_DIM = 64
NA_KH = 8
NA_KW = 16
NA_RPB_H = 15
NA_RPB_W = 31
PEER_HEADS = 8
PEER_NKEYS = 128
PEER_TOPK = 16
PEER_SEL = PEER_HEADS * PEER_TOPK
NEG_BIG = -1e30

LANES = 128
VMEM_LIMIT = 56 * 1024 * 1024

NT_DIMS = (((1,), (1,)), ((), ()))


def _cparams(sem):
    return pltpu.CompilerParams(dimension_semantics=sem, vmem_limit_bytes=VMEM_LIMIT)


def _rms(x, g):
    return x * lax.rsqrt(jnp.mean(x * x, axis=-1, keepdims=True) + RMS_EPS) * g


def _full_spec(shape):
    n = len(shape)
    return pl.BlockSpec(shape, lambda *_: (0,) * n)


def _mla_pre_kernel(x_ref, g_ref, wd_ref, qn_ref, kvn_ref, wuq_ref, wukv_ref, c_ref, s_ref,
                    q_out, k_out, v_out, *, scale):
    h = _rms(x_ref[...], g_ref[...]).astype(BF16)
    d = jnp.dot(h, wd_ref[...], preferred_element_type=F32)
    cq = _rms(d[:, :MLA_Q_LORA], qn_ref[...]).astype(BF16)
    ckv = _rms(d[:, MLA_Q_LORA:MLA_Q_LORA + MLA_KV_LORA], kvn_ref[...]).astype(BF16)
    c = c_ref[...]
    s = s_ref[...]
    o = MLA_Q_LORA + MLA_KV_LORA
    kpe = (d[:, o:o + LANES] * c + d[:, o + LANES:o + 2 * LANES] * s).astype(BF16)
    q = jnp.dot(cq, wuq_ref[...], preferred_element_type=F32)
    kv = jnp.dot(ckv, wukv_ref[...], preferred_element_type=F32)
    hw = MLA_HEADS * LANES
    for hh in range(MLA_HEADS):
        lo = hh * MLA_QK_PAD
        sl = slice(hh * LANES, (hh + 1) * LANES)
        q_out[:, lo:lo + LANES] = (q[:, sl] * scale).astype(BF16)
        a = q[:, hw + hh * LANES:hw + (hh + 1) * LANES]
        b = q[:, 2 * hw + hh * LANES:2 * hw + (hh + 1) * LANES]
        q_out[:, lo + LANES:lo + 2 * LANES] = ((a * c + b * s) * scale).astype(BF16)
        k_out[:, lo:lo + LANES] = kv[:, sl].astype(BF16)
        k_out[:, lo + LANES:lo + 2 * LANES] = kpe
    v_out[...] = kv[:, hw:].astype(BF16)


def _mla_pre(x, g, wd, qn, kvn, wuq, wukv, ctab, stab, tm):
    n = x.shape[0]
    scale = float((MLA_NOPE + MLA_ROPE) ** -0.5 * math.log2(math.e))
    row = lambda w: pl.BlockSpec((tm, w), lambda i: (i, 0))
    return pl.pallas_call(
        functools.partial(_mla_pre_kernel, scale=scale),
        grid=(n // tm,),
        in_specs=[row(D_MODEL), _full_spec(g.shape), _full_spec(wd.shape), _full_spec(qn.shape),
                  _full_spec(kvn.shape), _full_spec(wuq.shape), _full_spec(wukv.shape),
                  row(LANES), row(LANES)],
        out_specs=[row(MLA_HEADS * MLA_QK_PAD), row(MLA_HEADS * MLA_QK_PAD), row(MLA_HEADS * MLA_V)],
        out_shape=[jax.ShapeDtypeStruct((n, MLA_HEADS * MLA_QK_PAD), BF16),
                   jax.ShapeDtypeStruct((n, MLA_HEADS * MLA_QK_PAD), BF16),
                   jax.ShapeDtypeStruct((n, MLA_HEADS * MLA_V), BF16)],
        compiler_params=_cparams(("arbitrary",)),
        name="mla_pre",
    )(x, g, wd, qn, kvn, wuq, wukv, ctab, stab)


FLASH_MAX_UNROLL = 16


def _flash_kernel(q_ref, k_ref, v_ref, o_ref, sa_ref, sb_ref, *, tk, nk, unroll):
    q = q_ref[...]
    tq = q.shape[0]

    def scores(j, dst):
        start = pl.multiple_of(j * tk, tk)
        dst[...] = lax.dot_general(q, k_ref[pl.ds(start, tk), :], NT_DIMS, preferred_element_type=F32)

    def update(j, src, m, l, acc):
        start = pl.multiple_of(j * tk, tk)
        vs = v_ref[pl.ds(start, tk), :]
        s = src[...]
        m_new = jnp.maximum(m, jnp.max(s, axis=1, keepdims=True))
        alpha = jnp.exp2(m - m_new)
        p = jnp.exp2(s - m_new)
        l = alpha * l + jnp.sum(p, axis=1, keepdims=True)
        acc = alpha * acc + jnp.dot(p.astype(BF16), vs, preferred_element_type=F32)
        return m_new, l, acc

    bufs = (sa_ref, sb_ref)

    def body(jj, carry):
        m, l, acc = carry
        j = unroll * jj
        for u in range(unroll):
            nxt = j + u + 1
            if u == unroll - 1:
                nxt = jnp.minimum(nxt, nk - 1)
            scores(nxt, bufs[(u + 1) % 2])
            m, l, acc = update(j + u, bufs[u % 2], m, l, acc)
        return m, l, acc

    m0 = jnp.full((tq, 1), -jnp.inf, F32)
    l0 = jnp.zeros((tq, 1), F32)
    acc0 = jnp.zeros((tq, MLA_V), F32)
    scores(0, sa_ref)
    _, l, acc = lax.fori_loop(0, nk // unroll, body, (m0, l0, acc0))
    o_ref[...] = (acc / l).astype(BF16)


def _mla_attention(q, k, v, tok_off, batch, t_len, tq, tk):
    assert tok_off % t_len == 0 and t_len % tq == 0 and t_len % tk == 0
    nq = t_len // tq
    nk = t_len // tk
    unroll = math.gcd(nk, FLASH_MAX_UNROLL)
    assert unroll % 2 == 0
    qoff = tok_off // tq
    koff = tok_off // t_len
    return pl.pallas_call(
        functools.partial(_flash_kernel, tk=tk, nk=nk, unroll=unroll),
        grid=(batch, MLA_HEADS, nq),
        scratch_shapes=[pltpu.VMEM((tq, tk), F32), pltpu.VMEM((tq, tk), F32)],
        in_specs=[pl.BlockSpec((tq, MLA_QK_PAD), lambda b, h, i: (qoff + b * nq + i, h)),
                  pl.BlockSpec((t_len, MLA_QK_PAD), lambda b, h, i: (koff + b, h)),
                  pl.BlockSpec((t_len, MLA_V), lambda b, h, i: (koff + b, h))],
        out_specs=pl.BlockSpec((tq, MLA_V), lambda b, h, i: (b * nq + i, h)),
        out_shape=jax.ShapeDtypeStruct((batch * t_len, MLA_HEADS * MLA_V), BF16),
        compiler_params=_cparams(("arbitrary", "arbitrary", "arbitrary")),
        name="mla_flash",
    )(q, k, v)


def _na_pre_kernel(x_ref, g_ref, w_ref, b_ref, q_out, kv_out, *, scale):
    h = _rms(x_ref[...], g_ref[...]).astype(BF16)
    qkv = jnp.dot(h, w_ref[...], preferred_element_type=F32) + b_ref[...]
    q_out[...] = (qkv[:, :D_MODEL] * scale).astype(BF16)
    kv_out[...] = qkv[:, D_MODEL:].astype(BF16)


def _na_pre(x, g, w, b, tm):
    n = x.shape[0]
    row = lambda c: pl.BlockSpec((tm, c), lambda i: (i, 0))
    return pl.pallas_call(
        functools.partial(_na_pre_kernel, scale=float(NA_HEAD_DIM ** -0.5)),
        grid=(n // tm,),
        in_specs=[row(D_MODEL), _full_spec(g.shape), _full_spec(w.shape), _full_spec(b.shape)],
        out_specs=[row(D_MODEL), row(2 * D_MODEL)],
        out_shape=[jax.ShapeDtypeStruct((n, D_MODEL), BF16), jax.ShapeDtypeStruct((n, 2 * D_MODEL), BF16)],
        compiler_params=_cparams(("arbitrary",)),
        name="na_pre",
    )(x, g, w, b)


def _na_bias_kernel(rpb_ref, oh_ref, valid_ref, o_ref):
    rpb = rpb_ref[...]
    acc = jnp.zeros(o_ref.shape, F32)
    for d in range(NA_RPB_W):
        acc = jnp.where(oh_ref[d:d + 1, :] > 0, rpb[:, d:d + 1], acc)
    o_ref[...] = jnp.where(valid_ref[...] > 0, acc, NEG_BIG)


def _na_bias_table(rpb):
    c = jnp.arange(GRID_W)[:, None]
    kc = jnp.arange(GRID_W)[None, :]
    rel = (kc - c + NA_KW - 1).reshape(1, GRID_W * GRID_W)
    oh = (rel == jnp.arange(32)[:, None]).astype(F32)
    c0 = jnp.clip(c - NA_KW // 2, 0, GRID_W - NA_KW)
    valid = ((kc >= c0) & (kc < c0 + NA_KW)).astype(F32).reshape(1, GRID_W * GRID_W)
    rpb2 = jnp.pad(rpb.reshape(NA_HEADS * NA_RPB_H, NA_RPB_W), ((0, 0), (0, 1)))
    toep = pl.pallas_call(
        _na_bias_kernel,
        out_shape=jax.ShapeDtypeStruct((NA_HEADS * NA_RPB_H, GRID_W * GRID_W), F32),
        name="na_bias",
    )(rpb2, oh, valid)
    toep = toep.reshape(NA_HEADS, NA_RPB_H, GRID_W, GRID_W)
    tabs = []
    for base in range(NA_KH):
        t = toep[:, base:base + NA_KH]
        tabs.append(t.transpose(0, 2, 1, 3).reshape(NA_HEADS * GRID_W, NA_KH * GRID_W))
    return jnp.stack(tabs)


def _na_attn_kernel(*refs):
    q_ref = refs[0]
    kv_refs = refs[1:1 + NA_KH]
    b_ref = refs[1 + NA_KH]
    o_ref = refs[2 + NA_KH]
    lane = lax.broadcasted_iota(jnp.int32, (GRID_W, LANES), 1)
    lo = lane < NA_HEAD_DIM
    zero = jnp.zeros((GRID_W, LANES), BF16)
    pairs = range(NA_HEADS // 2)
    scores = []
    for hp in pairs:
        sl = slice(hp * LANES, (hp + 1) * LANES)
        q2 = q_ref[:, sl]
        qq = jnp.concatenate([jnp.where(lo, q2, zero), jnp.where(lo, zero, q2)], axis=0)
        kw = jnp.concatenate([r[:, sl] for r in kv_refs], axis=0)
        scores.append(lax.dot_general(qq, kw, NT_DIMS, preferred_element_type=F32) + b_ref[0, sl, :])
    probs, sums = [], []
    for s in scores:
        p = jnp.exp(s - jnp.max(s, axis=1, keepdims=True))
        sums.append(jnp.sum(p, axis=1, keepdims=True))
        probs.append(p.astype(BF16))
    for hp in pairs:
        sl = slice(hp * LANES, (hp + 1) * LANES)
        vw = jnp.concatenate([r[:, D_MODEL + hp * LANES:D_MODEL + (hp + 1) * LANES] for r in kv_refs], axis=0)
        o = jnp.dot(probs[hp], vw, preferred_element_type=F32) / sums[hp]
        o_ref[:, sl] = jnp.where(lo, o[:GRID_W], o[GRID_W:]).astype(BF16)


def _na_attention(q, kv, bias, tok_off, batch, t_len):
    rows = t_len // GRID_W
    assert rows >= NA_KH and tok_off % GRID_W == 0
    roff = tok_off // GRID_W

    def r0(r):
        return jnp.clip(r - NA_KH // 2, 0, rows - NA_KH)

    blk = (GRID_W, D_MODEL)
    q_spec = pl.BlockSpec(blk, lambda b, r: (roff + b * rows + r, 0))
    kv_specs = [pl.BlockSpec((GRID_W, 2 * D_MODEL),
                             functools.partial(lambda b, r, j: (roff + b * rows + r0(r) + j, 0), j=j))
                for j in range(NA_KH)]
    b_spec = pl.BlockSpec((1,) + bias.shape[1:], lambda b, r: (r0(r) - r + NA_KH - 1, 0, 0))
    return pl.pallas_call(
        _na_attn_kernel,
        grid=(batch, rows),
        in_specs=[q_spec] + kv_specs + [b_spec],
        out_specs=pl.BlockSpec(blk, lambda b, r: (b * rows + r, 0)),
        out_shape=jax.ShapeDtypeStruct((batch * t_len, D_MODEL), BF16),
        compiler_params=_cparams(("arbitrary", "arbitrary")),
        name="na_attn",
    )(q, *([kv] * NA_KH), bias)


def _mix_post_kernel(x_ref, o_ref, wo_ref, g_ref, wq_ref, x_out, h_out, q_out):
    x = x_ref[...] + jnp.dot(o_ref[...], wo_ref[...], preferred_element_type=F32)
    x_out[...] = x
    h = _rms(x, g_ref[...]).astype(BF16)
    h_out[...] = h
    q = jnp.dot(h, wq_ref[...], preferred_element_type=F32).astype(BF16)
    for c in range(q_out.shape[0]):
        q_out[c] = q[:, c * LANES:(c + 1) * LANES]


def _mix_post(x, o, wo, g, wq, tm):
    n = x.shape[0]
    row = lambda w: pl.BlockSpec((tm, w), lambda i: (i, 0))
    nq = wq.shape[1] // LANES
    return pl.pallas_call(
        _mix_post_kernel,
        grid=(n // tm,),
        in_specs=[row(D_MODEL), row(D_MODEL), _full_spec(wo.shape), _full_spec(g.shape), _full_spec(wq.shape)],
        out_specs=[row(D_MODEL), row(D_MODEL), pl.BlockSpec((nq, tm, LANES), lambda i: (0, i, 0))],
        out_shape=[jax.ShapeDtypeStruct((n, D_MODEL), F32),
                   jax.ShapeDtypeStruct((n, D_MODEL), BF16),
                   jax.ShapeDtypeStruct((nq, n, LANES), BF16)],
        compiler_params=_cparams(("arbitrary",)),
        name="mix_post",
    )(x, o, wo, g, wq)


IDX_NONE = float(1 << 20)
SUBLANES = 8


def _top16(s, rowid):
    ngrp = s.shape[0] // SUBLANES
    grp = [s[g * SUBLANES:(g + 1) * SUBLANES] for g in range(ngrp)]
    vals, poss = [], []
    for _ in range(PEER_TOPK):
        m8 = grp[0]
        for g in range(1, ngrp):
            m8 = jnp.maximum(m8, grp[g])
        m = jnp.max(m8, axis=0, keepdims=True)
        t = jnp.full(m8.shape, IDX_NONE, F32)
        for g in reversed(range(ngrp)):
            t = jnp.where(grp[g] == m, float(g * SUBLANES), t)
        first = jnp.min(t + rowid[0], axis=0, keepdims=True)
        vals.append(m)
        poss.append(first)
        grp = [jnp.where(rowid[g] == first, -jnp.inf, grp[g]) for g in range(ngrp)]
    return vals, poss


PEER_NCAND = 2 * PEER_TOPK + 6 * 8


def _cand_flat_index(pos):
    g = pos // SUBLANES
    sb = pos % SUBLANES
    return jnp.where(g < 2, pos, jnp.where(g < PEER_NCAND // SUBLANES - 1,
                                           (g - 1) * PEER_TOPK + sb, (SUBLANES + sb) * PEER_TOPK))


def _peer_route_kernel(q_ref, keys_ref, ia_out, ib_out, g_out, ia_scr, ib_scr, g_scr):
    tm = q_ref.shape[1]
    sub = lax.broadcasted_iota(jnp.int32, (SUBLANES, tm), 0).astype(F32)
    rowid = [sub + float(g * SUBLANES) for g in range(PEER_NKEYS // SUBLANES)]

    def head(h, _):
        halves = []
        for p in range(2):
            qh = q_ref[h * 2 + p]
            st = lax.dot_general(keys_ref[h * 2 + p], qh, NT_DIMS, preferred_element_type=F32)
            halves.append(_top16(st, rowid))
        (va, ia), (vb, ib) = halves
        va_all = jnp.concatenate(va, axis=0)
        vb_all = jnp.concatenate(vb, axis=0)
        cand = jnp.concatenate([va[0] + vb_all] + [va[i] + vb_all[:8] for i in range(1, 8)]
                               + [va_all[8:] + vb[0]], axis=0)
        fv, fp = _top16(cand, rowid)
        fv = jnp.concatenate(fv, axis=0)
        fi = _cand_flat_index(jnp.concatenate(fp, axis=0).astype(jnp.int32))
        fa = fi // PEER_TOPK
        fb = fi % PEER_TOPK
        sel_a = jnp.zeros(fv.shape, F32)
        sel_b = jnp.zeros(fv.shape, F32)
        for i in range(PEER_TOPK):
            sel_a = jnp.where(fa == i, ia[i], sel_a)
            sel_b = jnp.where(fb == i, ib[i], sel_b)
        e = jnp.exp(fv - fv[0:1])
        g = e / jnp.sum(e, axis=0, keepdims=True)
        row = pl.multiple_of(h * PEER_TOPK, PEER_TOPK)
        ia_scr[pl.ds(row, PEER_TOPK), :] = sel_a.astype(jnp.int32)
        ib_scr[pl.ds(row, PEER_TOPK), :] = sel_b.astype(jnp.int32)
        g_scr[pl.ds(row, PEER_TOPK), :] = g
        return 0

    lax.fori_loop(0, PEER_HEADS, head, 0)
    ia_out[...] = ia_scr[...].T
    ib_out[...] = ib_scr[...].T
    g_out[...] = g_scr[...].T


def _peer_route(qp, keys, tm):
    n = qp.shape[1]
    row = lambda w: pl.BlockSpec((tm, w), lambda i: (i, 0))
    return pl.pallas_call(
        _peer_route_kernel,
        grid=(n // tm,),
        in_specs=[pl.BlockSpec((qp.shape[0], tm, LANES), lambda i: (0, i, 0)), _full_spec(keys.shape)],
        out_specs=[row(PEER_SEL), row(PEER_SEL), row(PEER_SEL)],
        out_shape=[jax.ShapeDtypeStruct((n, PEER_SEL), jnp.int32),
                   jax.ShapeDtypeStruct((n, PEER_SEL), jnp.int32),
                   jax.ShapeDtypeStruct((n, PEER_SEL), F32)],
        scratch_shapes=[pltpu.VMEM((PEER_SEL, tm), jnp.int32),
                        pltpu.VMEM((PEER_SEL, tm), jnp.int32),
                        pltpu.VMEM((PEER_SEL, tm), F32)],
        compiler_params=_cparams(("arbitrary",)),
        name="peer_route",
    )(qp, keys)


W_PITCH = PEER_NKEYS + 8
W_BUILD_UNROLL = 32
HI_MASK = 0xFFFF0000
VMEM_LIMIT_DENSE = 60 * 1024 * 1024


def _gate_tile_bits(sub, ia, ib, g):
    pt = jnp.where(sub == ia, 1.0, 0.0).astype(BF16)
    qt = jnp.where(sub == ib, g, 0.0).astype(BF16)
    w = lax.dot_general(pt, qt, NT_DIMS, preferred_element_type=F32)
    return lax.bitcast_convert_type(w, jnp.uint32) + jnp.uint32(0x8000)


def _peer_dense_kernel(h_ref, ia_ref, ib_ref, g_ref, x_ref, u_ref, v_ref, gn_ref, o_ref, w_scr, acc_ref, *,
                       eb, final_norm):
    j = pl.program_id(1)
    tm = h_ref.shape[0]
    half = tm // 2
    nblk = eb // PEER_NKEYS

    @pl.when(j == 0)
    def _():
        acc_ref[...] = jnp.zeros_like(acc_ref)
        sub = lax.broadcasted_iota(jnp.int32, (PEER_NKEYS, PEER_SEL), 0)

        def build(cb, _):
            base = pl.multiple_of(cb * W_BUILD_UNROLL, W_BUILD_UNROLL)
            top = [r[pl.ds(base, W_BUILD_UNROLL), :] for r in (ia_ref, ib_ref, g_ref)]
            bot = [r[pl.ds(base + half, W_BUILD_UNROLL), :] for r in (ia_ref, ib_ref, g_ref)]
            for t in range(W_BUILD_UNROLL):
                row1 = slice(t, t + 1)
                w_top = _gate_tile_bits(sub, top[0][row1], top[1][row1], top[2][row1] * 0.5)
                w_bot = _gate_tile_bits(sub, bot[0][row1], bot[1][row1], bot[2][row1] * 0.5)
                row = pl.multiple_of((base + t) * W_PITCH, 8)
                w_scr[pl.ds(row, PEER_NKEYS), :] = (w_top & jnp.uint32(HI_MASK)) | (w_bot >> 16)
            return 0

        lax.fori_loop(0, half // W_BUILD_UNROLL, build, 0)

    a = lax.dot_general(h_ref[...], u_ref[...], NT_DIMS, preferred_element_type=F32)
    parts = []
    for k in range(nblk):
        bits = w_scr[pl.ds(j * nblk + k, half, stride=W_PITCH), :]
        wk = jnp.concatenate([lax.bitcast_convert_type(bits & jnp.uint32(HI_MASK), F32),
                              lax.bitcast_convert_type(bits << 16, F32)], axis=0)
        ak = a[:, k * PEER_NKEYS:(k + 1) * PEER_NKEYS]
        parts.append((ak * (1.0 + lax.erf(ak * (2.0 ** -0.5))) * wk).astype(BF16))
    wg = jnp.concatenate(parts, axis=1)
    acc_ref[...] += jnp.dot(wg, v_ref[...], preferred_element_type=F32)

    @pl.when(j == pl.num_programs(1) - 1)
    def _():
        y = x_ref[...] + acc_ref[...]
        o_ref[...] = _rms(y, gn_ref[...]) if final_norm else y


def _peer_dense(h, ia, ib, g, x, u, v, gain, final_norm, tm, eb):
    n = h.shape[0]
    ne = u.shape[0]
    assert tm % (2 * W_BUILD_UNROLL) == 0
    row = lambda w: pl.BlockSpec((tm, w), lambda i, j: (i, 0))
    x_spec = pl.BlockSpec((tm, D_MODEL), lambda i, j: (i, 0), pipeline_mode=pl.Buffered(1))
    return pl.pallas_call(
        functools.partial(_peer_dense_kernel, eb=eb, final_norm=final_norm),
        grid=(n // tm, ne // eb),
        in_specs=[row(D_MODEL), row(PEER_SEL), row(PEER_SEL), row(PEER_SEL), x_spec,
                  pl.BlockSpec((eb, D_MODEL), lambda i, j: (j, 0)),
                  pl.BlockSpec((eb, D_MODEL), lambda i, j: (j, 0)),
                  pl.BlockSpec(gain.shape, lambda i, j: (0, 0))],
        out_specs=row(D_MODEL),
        out_shape=jax.ShapeDtypeStruct((n, D_MODEL), F32),
        scratch_shapes=[pltpu.VMEM((tm // 2 * W_PITCH, PEER_NKEYS), jnp.uint32),
                        pltpu.VMEM((tm, D_MODEL), F32)],
        compiler_params=pltpu.CompilerParams(dimension_semantics=("arbitrary", "arbitrary"),
                                             vmem_limit_bytes=VMEM_LIMIT_DENSE),
        name="peer_dense",
    )(h, ia, ib, g, x, u, v, gain)


def _swap_halves(w):
    half = MLA_ROPE // 2
    return jnp.concatenate([w[..., half:], w[..., :half]], axis=-1)


def _pad_lanes(w):
    return jnp.pad(w, [(0, 0)] * (w.ndim - 1) + [(0, LANES - w.shape[-1])])


def _mla_weights(w_dq, w_uq, w_dkv, w_ukv):
    kpe = w_dkv[:, MLA_KV_LORA:]
    wd = jnp.concatenate([w_dq, w_dkv[:, :MLA_KV_LORA], _pad_lanes(kpe), _pad_lanes(_swap_halves(kpe))], axis=1)
    uq = w_uq.reshape(MLA_Q_LORA, MLA_HEADS, MLA_NOPE + MLA_ROPE)
    pe = uq[:, :, MLA_NOPE:]
    flat = lambda a: a.reshape(MLA_Q_LORA, -1)
    wuq = jnp.concatenate([flat(uq[:, :, :MLA_NOPE]), flat(_pad_lanes(pe)), flat(_pad_lanes(_swap_halves(pe)))], axis=1)
    ukv = w_ukv.reshape(MLA_KV_LORA, MLA_HEADS, MLA_NOPE + MLA_V)
    wukv = jnp.concatenate([ukv[:, :, :MLA_NOPE].reshape(MLA_KV_LORA, -1),
                            ukv[:, :, MLA_NOPE:].reshape(MLA_KV_LORA, -1)], axis=1)
    return wd.astype(BF16), wuq.astype(BF16), wukv.astype(BF16)


def _rope_tables(segments):
    inv = ROPE_THETA ** (-jnp.arange(0, MLA_ROPE, 2, dtype=F32) / MLA_ROPE)
    cs, ss = [], []
    for batch, t_len in segments:
        ang = jnp.arange(t_len, dtype=F32)[:, None] * inv[None, :]
        cos, sin = jnp.cos(ang), jnp.sin(ang)
        cs.append(jnp.tile(_pad_lanes(jnp.concatenate([cos, cos], axis=1)), (batch, 1)))
        ss.append(jnp.tile(_pad_lanes(jnp.concatenate([-sin, sin], axis=1)), (batch, 1)))
    return jnp.concatenate(cs), jnp.concatenate(ss)


TOKEN_TILE = 256
ROUTE_TILE = 1024
DENSE_TOKEN_TILE = 512
DENSE_EXPERT_TILE = 2048
FLASH_Q_TILE = 256
FLASH_KV_TILE = 1024


def _pick_tile(n, pref):
    t = pref
    while n % t:
        t //= 2
    return t


def _trunk(x_prompt, x_sample, norm_mix, norm_ffn, norm_final,
           mla_w_dq, mla_q_norm, mla_w_uq, mla_w_dkv, mla_kv_norm, mla_w_ukv, mla_w_o,
           na_w_qkv, na_b_qkv, na_rpb, na_w_o, peer_w_q, peer_sub_keys, peer_u, peer_v):
    segs = [(x_prompt.shape[0], x_prompt.shape[1]), (x_sample.shape[0], x_sample.shape[1])]
    offs = [0, segs[0][0] * segs[0][1]]
    x = jnp.concatenate([x_prompt.reshape(-1, D_MODEL), x_sample.reshape(-1, D_MODEL)], axis=0)
    n = x.shape[0]
    tm = _pick_tile(n, TOKEN_TILE)
    ctab, stab = _rope_tables(segs)
    row2 = lambda a: a.reshape(1, -1)

    for i in range(DEPTH):
        j = i // 2
        if i % 2 == 0:
            wd, wuq, wukv = _mla_weights(mla_w_dq[j], mla_w_uq[j], mla_w_dkv[j], mla_w_ukv[j])
            q, k, v = _mla_pre(x, row2(norm_mix[i]), wd, row2(mla_q_norm[j]), row2(mla_kv_norm[j]),
                               wuq, wukv, ctab, stab, tm)
            o = jnp.concatenate([
                _mla_attention(q, k, v, off, b, t, _pick_tile(t, FLASH_Q_TILE), _pick_tile(t // 2, FLASH_KV_TILE))
                for off, (b, t) in zip(offs, segs)], axis=0)
            wo = mla_w_o[j].astype(BF16)
        else:
            q, kv = _na_pre(x, row2(norm_mix[i]), na_w_qkv[j].astype(BF16), row2(na_b_qkv[j]), tm)
            bias = _na_bias_table(na_rpb[j])
            o = jnp.concatenate([
                _na_attention(q, kv, bias, off, b, t) for off, (b, t) in zip(offs, segs)], axis=0)
            wo = na_w_o[j].astype(BF16)
        x, h, qp = _mix_post(x, o, wo, row2(norm_ffn[i]), peer_w_q[i].astype(BF16), tm)
        keys = peer_sub_keys[i].reshape(PEER_HEADS * 2, PEER_NKEYS, -1).astype(BF16)
        ia, ib, g = _peer_route(qp, keys, _pick_tile(n, ROUTE_TILE))
        x = _peer_dense(h, ia, ib, g, x, peer_u[i].astype(BF16), peer_v[i].astype(BF16),
                        row2(norm_final), i == DEPTH - 1, _pick_tile(n, DENSE_TOKEN_TILE), DENSE_EXPERT_TILE)

    return x[:offs[1]].reshape(x_prompt.shape), x[offs[1]:].reshape(x_sample.shape)


def kernel(x_prompt, x_sample, norm_mix, norm_ffn, norm_final, mla_w_dq, mla_q_norm, mla_w_uq, mla_w_dkv, mla_kv_norm, mla_w_ukv, mla_w_o, na_w_qkv, na_b_qkv, na_rpb, na_w_o, peer_w_q, peer_sub_keys, peer_u, peer_v):
    return _trunk(x_prompt, x_sample, norm_mix, norm_ffn, norm_final,
                  mla_w_dq, mla_q_norm, mla_w_uq, mla_w_dkv, mla_kv_norm, mla_w_ukv, mla_w_o,
                  na_w_qkv, na_b_qkv, na_rpb, na_w_o, peer_w_q, peer_sub_keys, peer_u, peer_v)
```
